```python
import jax, jax.numpy as jnp
from jax import lax
import numpy as np

D_MODEL = 4096
BATCH = 2
SEQ = 4096
DEPTH = 2
DEC_BATCH = 32
DEC_SEQ = 64
PAST_LEN = 1024

CHUNK = 64
D_MIX = D_MODEL
D_CONV = D_MIX // 2
N_CONV_GROUPS = 16
D_SMLP = D_MIX - D_CONV
N_SMLP_HEADS = 16
SMLP_HEAD_DIM = D_SMLP // N_SMLP_HEADS
SMLP_CHUNK = 128
CONV_WIDTH = 3
D_IN_PROJ = 3 * D_CONV + 2 * D_SMLP
D_FF = 14336
N_EXPERTS = 8
TOP_K = 2
N_DENSE = (DEPTH + 1) // 2
N_MOE = DEPTH // 2
EPS = 1e-5

kernel_name = "hybrid_shortconv_spatialmlp_stream_step"


def rms_norm(x, g):
    xf = x.astype(jnp.float32)
    xf = xf * lax.rsqrt(jnp.mean(xf * xf, axis=-1, keepdims=True) + EPS)
    return (xf * g.astype(jnp.float32)).astype(x.dtype)


def group_rms_norm(y, g, n_groups):
    shp = y.shape
    yg = y.reshape(shp[:-1] + (n_groups, shp[-1] // n_groups)).astype(jnp.float32)
    yg = yg * lax.rsqrt(jnp.mean(yg * yg, axis=-1, keepdims=True) + EPS)
    return (yg.reshape(shp) * g.astype(jnp.float32)).astype(y.dtype)


def layer_norm(x, g, b):
    xf = x.astype(jnp.float32)
    xc = xf - jnp.mean(xf, axis=-1, keepdims=True)
    var = jnp.mean(xc * xc, axis=-1, keepdims=True)
    return (xc * lax.rsqrt(var + EPS) * g.astype(jnp.float32) + b.astype(jnp.float32)).astype(x.dtype)


def short_conv(h, prev, w, b):
    T = h.shape[1]
    hp = jnp.concatenate([prev.astype(h.dtype), h], axis=1)
    out = b.astype(h.dtype)
    for k in range(CONV_WIDTH):
        out = out + w[k] * hp[:, k:k + T]
    return out, hp[:, -(CONV_WIDTH - 1):]


def spatial_gating(u, v, ws, wb):
    bn, T, _ = v.shape
    L = min(T, SMLP_CHUNK)
    nc = T // L
    mask = jnp.tril(jnp.ones((L, L), dtype=bool))
    w = jnp.where(mask, ws[:, :L, :L], 0).astype(v.dtype)
    vc = v.reshape(bn, nc, L, N_SMLP_HEADS, SMLP_HEAD_DIM)
    s = jnp.einsum('hij,bcjhd->bcihd', w, vc) + wb[:, :L].T[None, None, :, :, None]
    return u * s.reshape(bn, T, D_SMLP)


def mixer(xn, conv_prev, w_in, conv_w, conv_b, v_ln_g, v_ln_b, ws, wb, g_out_a, g_out_b, w_out):
    z = xn @ w_in
    gb, gc, h, u, v = jnp.split(z, [D_CONV, 2 * D_CONV, 3 * D_CONV, 3 * D_CONV + D_SMLP], axis=-1)
    conv_out, conv_state = short_conv(gc * h, conv_prev, conv_w, conv_b)
    y_a = gb * conv_out
    vn = layer_norm(v, v_ln_g, v_ln_b)
    y_b = spatial_gating(u, vn, ws, wb)
    y = jnp.concatenate([group_rms_norm(y_a, g_out_a, N_CONV_GROUPS),
                         group_rms_norm(y_b, g_out_b, N_SMLP_HEADS)], axis=-1)
    return y @ w_out, conv_state, vn


def swiglu(x, wg, wu, wd):
    return (jax.nn.silu(x @ wg) * (x @ wu)) @ wd


def moe_swiglu(x, w_router, b_router, wg, wu, wd):
    shp = x.shape
    xt = x.reshape(-1, shp[-1])
    logits = (xt @ w_router).astype(jnp.float32) + b_router.astype(jnp.float32)
    top_v, top_i = lax.top_k(logits, TOP_K)
    top_w = jax.nn.softmax(top_v, axis=-1)
    gates = jnp.sum(jax.nn.one_hot(top_i, N_EXPERTS, dtype=jnp.float32) * top_w[..., None],
                    axis=1).astype(x.dtype)
    out = jnp.zeros_like(xt)
    for e in range(N_EXPERTS):
        out = out + gates[:, e:e + 1] * swiglu(xt, wg[e], wu[e], wd[e])
    return out.reshape(shp)


def trunk(x, conv_prev, ln1_g, w_in, conv_w, conv_b, v_ln_g, v_ln_b, smlp_w, smlp_b,
          out_norm_a_g, out_norm_b_g, w_out, ln2_g, ffn_w_gate, ffn_w_up, ffn_w_down,
          router_w, router_b, moe_w_gate, moe_w_up, moe_w_down, final_norm_g):
    conv_states, v_rows = [], []
    for l in range(DEPTH):
        y, cs, vn = mixer(rms_norm(x, ln1_g[l]), conv_prev[l], w_in[l], conv_w[l], conv_b[l],
                          v_ln_g[l], v_ln_b[l], smlp_w[l], smlp_b[l],
                          out_norm_a_g[l], out_norm_b_g[l], w_out[l])
        x = x + y
        xn = rms_norm(x, ln2_g[l])
        j = l // 2
        if l % 2 == 0:
            x = x + swiglu(xn, ffn_w_gate[j], ffn_w_up[j], ffn_w_down[j])
        else:
            x = x + moe_swiglu(xn, router_w[j], router_b[j], moe_w_gate[j], moe_w_up[j], moe_w_down[j])
        conv_states.append(cs)
        v_rows.append(vn)
    return rms_norm(x, final_norm_g), jnp.stack(conv_states), jnp.stack(v_rows)


def setup_inputs(seed: int = 0) -> dict:
    key = jax.random.key(seed)
    ks = jax.random.split(key, 24)
    nrm = jax.random.normal
    f32 = jnp.float32
    return {
        "x_prompt": nrm(ks[0], (BATCH, SEQ, D_MODEL), f32),
        "x_sample": nrm(ks[1], (DEC_BATCH, DEC_SEQ, D_MODEL), f32),
        "state_conv": nrm(ks[2], (DEPTH, DEC_BATCH, CONV_WIDTH - 1, D_CONV), f32),
        "ln1_g": 1.0 + 0.02 * nrm(ks[3], (DEPTH, D_MODEL), f32),
        "w_in": nrm(ks[4], (DEPTH, D_MODEL, D_IN_PROJ), f32) * D_MODEL ** -0.5,
        "conv_w": nrm(ks[5], (DEPTH, CONV_WIDTH, D_CONV), f32) * CONV_WIDTH ** -0.5,
        "conv_b": 0.01 * nrm(ks[6], (DEPTH, D_CONV), f32),
        "v_ln_g": 1.0 + 0.02 * nrm(ks[7], (DEPTH, D_SMLP), f32),
        "v_ln_b": 0.01 * nrm(ks[8], (DEPTH, D_SMLP), f32),
        "smlp_w": nrm(ks[9], (DEPTH, N_SMLP_HEADS, SMLP_CHUNK, SMLP_CHUNK), f32) * SMLP_CHUNK ** -0.5,
        "smlp_b": 1.0 + 0.02 * nrm(ks[10], (DEPTH, N_SMLP_HEADS, SMLP_CHUNK), f32),
        "out_norm_a_g": 1.0 + 0.02 * nrm(ks[11], (DEPTH, D_CONV), f32),
        "out_norm_b_g": 1.0 + 0.02 * nrm(ks[12], (DEPTH, D_SMLP), f32),
        "w_out": nrm(ks[13], (DEPTH, D_MIX, D_MODEL), f32) * D_MIX ** -0.5,
        "ln2_g": 1.0 + 0.02 * nrm(ks[14], (DEPTH, D_MODEL), f32),
        "ffn_w_gate": nrm(ks[15], (N_DENSE, D_MODEL, D_FF), f32) * D_MODEL ** -0.5,
        "ffn_w_up": nrm(ks[16], (N_DENSE, D_MODEL, D_FF), f32) * D_MODEL ** -0.5,
        "ffn_w_down": nrm(ks[17], (N_DENSE, D_FF, D_MODEL), f32) * D_FF ** -0.5,
        "router_w": nrm(ks[18], (N_MOE, D_MODEL, N_EXPERTS), f32) * D_MODEL ** -0.5,
        "router_b": 0.01 * nrm(ks[19], (N_MOE, N_EXPERTS), f32),
        "moe_w_gate": nrm(ks[20], (N_MOE, N_EXPERTS, D_MODEL, D_FF), f32) * D_MODEL ** -0.5,
        "moe_w_up": nrm(ks[21], (N_MOE, N_EXPERTS, D_MODEL, D_FF), f32) * D_MODEL ** -0.5,
        "moe_w_down": nrm(ks[22], (N_MOE, N_EXPERTS, D_FF, D_MODEL), f32) * D_FF ** -0.5,
        "final_norm_g": 1.0 + 0.02 * nrm(ks[23], (D_MODEL,), f32),
    }


def reference(x_prompt, x_sample, state_conv, ln1_g, w_in, conv_w, conv_b, v_ln_g, v_ln_b,
              smlp_w, smlp_b, out_norm_a_g, out_norm_b_g, w_out, ln2_g, ffn_w_gate, ffn_w_up,
              ffn_w_down, router_w, router_b, moe_w_gate, moe_w_up, moe_w_down, final_norm_g):
    weights = (ln1_g, w_in, conv_w, conv_b, v_ln_g, v_ln_b, smlp_w, smlp_b, out_norm_a_g,
               out_norm_b_g, w_out, ln2_g, ffn_w_gate, ffn_w_up, ffn_w_down, router_w, router_b,
               moe_w_gate, moe_w_up, moe_w_down, final_norm_g)
    conv_zero = jnp.zeros((DEPTH, x_prompt.shape[0], CONV_WIDTH - 1, D_CONV), x_prompt.dtype)
    y_prompt, new_conv_prompt, _ = trunk(x_prompt, conv_zero, *weights)
    y_sample, new_conv_sample, new_v_sample = trunk(x_sample, state_conv, *weights)
    return (y_prompt, y_sample, new_conv_prompt, new_conv_sample, new_v_sample)
```

```python
import functools

import jax
import jax.numpy as jnp
from jax import lax
from jax.experimental import pallas as pl
from jax.experimental.pallas import tpu as pltpu

D_MODEL = 4096
BATCH = 2
SEQ = 4096
DEPTH = 2
DEC_BATCH = 32
DEC_SEQ = 64
D_CONV = 2048
D_SMLP = 2048
N_GROUPS = 16
GROUP = 128
SMLP_CHUNK = 128
D_IN_PROJ = 3 * D_CONV + 2 * D_SMLP
D_FF = 14336
N_EXPERTS = 8
TOP_K = 2
EPS = 1e-5

N_PROMPT = BATCH * SEQ
N_SAMPLE = DEC_BATCH * DEC_SEQ
N_TOK = N_PROMPT + N_SAMPLE
ROWS = SMLP_CHUNK
N_PROMPT_BLOCKS = N_PROMPT // ROWS
N_SAMPLE_BLOCKS = N_SAMPLE // ROWS
N_BLOCKS = N_PROMPT_BLOCKS + N_SAMPLE_BLOCKS
BLOCKS_PER_SEQ = SEQ // ROWS

V7X_VMEM_LIMIT_BYTES = 56 * 1024 * 1024

TM = 1024
TN_PROJ = 512
TF = 256
TR = 256
TG = 256
P_ROWS = TOP_K * N_TOK + N_EXPERTS * TM
N_MOE_BLOCKS = P_ROWS // TM

bf16 = jnp.bfloat16
f32 = jnp.float32


def _params(n_axes):
    return pltpu.CompilerParams(dimension_semantics=("arbitrary",) * n_axes,
                                vmem_limit_bytes=V7X_VMEM_LIMIT_BYTES)


def _rms(x, g):
    ms = jnp.mean(x * x, axis=-1, keepdims=True)
    return x * lax.rsqrt(ms + EPS) * g


def _dot(a, w):
    return jnp.dot(a, w, preferred_element_type=f32)


def _rms_kernel(x_ref, g_ref, o_ref):
    o_ref[...] = _rms(x_ref[...], g_ref[...]).astype(o_ref.dtype)


def rms_norm(x, g, out_dtype):
    n = x.shape[0]
    return pl.pallas_call(
        _rms_kernel,
        grid=(n // TR,),
        in_specs=[pl.BlockSpec((TR, D_MODEL), lambda i: (i, 0)),
                  pl.BlockSpec((1, D_MODEL), lambda i: (0, 0))],
        out_specs=pl.BlockSpec((TR, D_MODEL), lambda i: (i, 0)),
        out_shape=jax.ShapeDtypeStruct((n, D_MODEL), out_dtype),
        compiler_params=_params(1), name="rms_norm",
    )(x, g.reshape(1, D_MODEL))


def _add_rms_kernel(x_ref, d_ref, g_ref, xo_ref, no_ref):
    x = x_ref[...] + d_ref[...]
    xo_ref[...] = x
    no_ref[...] = _rms(x, g_ref[...]).astype(no_ref.dtype)


def add_rms_norm(x, d, g):
    n = x.shape[0]
    row = pl.BlockSpec((TR, D_MODEL), lambda i: (i, 0))
    return pl.pallas_call(
        _add_rms_kernel,
        grid=(n // TR,),
        in_specs=[row, row, pl.BlockSpec((1, D_MODEL), lambda i: (0, 0))],
        out_specs=[row, row],
        out_shape=[jax.ShapeDtypeStruct((n, D_MODEL), f32),
                   jax.ShapeDtypeStruct((n, D_MODEL), bf16)],
        compiler_params=_params(1), name="add_rms_norm",
    )(x, d, g.reshape(1, D_MODEL))


def _proj_kernel(a_ref, w_ref, o_ref):
    o_ref[...] = _dot(a_ref[...], w_ref[...].astype(bf16)).astype(o_ref.dtype)


def in_proj(a, w_all, layer):
    n = a.shape[0]
    return pl.pallas_call(
        _proj_kernel,
        grid=(n // TM, D_IN_PROJ // TN_PROJ),
        in_specs=[pl.BlockSpec((TM, D_MODEL), lambda i, j: (i, 0)),
                  pl.BlockSpec((None, D_MODEL, TN_PROJ), lambda i, j: (layer, 0, j))],
        out_specs=pl.BlockSpec((TM, TN_PROJ), lambda i, j: (i, j)),
        out_shape=jax.ShapeDtypeStruct((n, D_IN_PROJ), f32),
        compiler_params=_params(2), name="in_proj",
    )(a, w_all)


def _proj_res_kernel(a_ref, w_ref, x_ref, o_ref):
    o_ref[...] = x_ref[...] + _dot(a_ref[...], w_ref[...].astype(bf16))


def out_proj_residual(a, w_all, layer, x):
    n = a.shape[0]
    tile = pl.BlockSpec((TM, TN_PROJ), lambda i, j: (i, j))
    return pl.pallas_call(
        _proj_res_kernel,
        grid=(n // TM, D_MODEL // TN_PROJ),
        in_specs=[pl.BlockSpec((TM, D_MODEL), lambda i, j: (i, 0)),
                  pl.BlockSpec((None, D_MODEL, TN_PROJ), lambda i, j: (layer, 0, j)),
                  tile],
        out_specs=tile,
        out_shape=jax.ShapeDtypeStruct((n, D_MODEL), f32),
        compiler_params=_params(2), name="out_proj",
    )(a, w_all, x)


def _mixer_kernel(z_ref, st_ref, cw_ref, cb_ref, lg_ref, lb_ref, w_ref, sb_ref, ga_ref, gb_ref,
                  y_ref, vn_ref, tail_ref, prev_ref):
    i = pl.program_id(0)

    @pl.when(i == 0)
    def _():
        prev_ref[...] = jnp.zeros_like(prev_ref)

    is_sample = i >= N_PROMPT_BLOCKS
    sample_row = jnp.full((1, GROUP), i, jnp.int32) >= N_PROMPT_BLOCKS
    seq_start_row = (jnp.full((1, GROUP), i, jnp.int32) % BLOCKS_PER_SEQ) == 0
    mid = jnp.where(is_sample, DEC_SEQ, -8)
    row = lax.broadcasted_iota(jnp.int32, (ROWS, GROUP), 0)
    col = lax.broadcasted_iota(jnp.int32, (ROWS, GROUP), 1)
    sample_full = jnp.full((ROWS, GROUP), i, jnp.int32) >= N_PROMPT_BLOCKS
    same_seq = jnp.logical_or(jnp.logical_not(sample_full), (row // DEC_SEQ) == (col // DEC_SEQ))
    causal = jnp.logical_and(col <= row, same_seq)

    def group_rms(y, g):
        ms = jnp.mean(y * y, axis=-1, keepdims=True)
        return y * lax.rsqrt(ms + EPS) * g

    for grp in range(N_GROUPS):
        sl = slice(grp * GROUP, (grp + 1) * GROUP)
        gate_b = z_ref[:, grp * GROUP:(grp + 1) * GROUP]
        cin = (z_ref[:, D_CONV + grp * GROUP:D_CONV + (grp + 1) * GROUP]
               * z_ref[:, 2 * D_CONV + grp * GROUP:2 * D_CONV + (grp + 1) * GROUP])
        zero = jnp.zeros((1, GROUP), f32)
        a1 = jnp.where(sample_row, st_ref[0, 1:2, sl], jnp.where(seq_start_row, zero, prev_ref[7:8, sl]))
        a2 = jnp.where(sample_row, st_ref[0, 0:1, sl], jnp.where(seq_start_row, zero, prev_ref[6:7, sl]))
        b1 = st_ref[0, 3:4, sl]
        b2 = st_ref[0, 2:3, sl]
        s1 = pltpu.roll(cin, 1, 0)
        s2 = pltpu.roll(cin, 2, 0)
        s1 = jnp.where(row == 0, a1, s1)
        s1 = jnp.where(row == mid, b1, s1)
        s2 = jnp.where(row == 0, a2, jnp.where(row == 1, a1, s2))
        s2 = jnp.where(row == mid, b2, jnp.where(row == mid + 1, b1, s2))
        conv = cb_ref[:, sl] + cw_ref[0:1, sl] * s2 + cw_ref[1:2, sl] * s1 + cw_ref[2:3, sl] * cin
        y_ref[:, sl] = group_rms(gate_b * conv, ga_ref[:, sl]).astype(y_ref.dtype)
        tail_ref[0:8, sl] = cin[DEC_SEQ - 8:DEC_SEQ]
        tail_ref[8:16, sl] = cin[ROWS - 8:ROWS]
        prev_ref[:, sl] = cin[ROWS - 8:ROWS]

    v0 = 3 * D_CONV + D_SMLP
    u0 = 3 * D_CONV
    acc = z_ref[:, v0:v0 + GROUP]
    for grp in range(1, N_GROUPS):
        acc = acc + z_ref[:, v0 + grp * GROUP:v0 + (grp + 1) * GROUP]
    mean = jnp.sum(acc, axis=-1, keepdims=True) * (1.0 / D_SMLP)
    acc = jnp.zeros((ROWS, GROUP), f32)
    for grp in range(N_GROUPS):
        vc = z_ref[:, v0 + grp * GROUP:v0 + (grp + 1) * GROUP] - mean
        acc = acc + vc * vc
    rstd = lax.rsqrt(jnp.sum(acc, axis=-1, keepdims=True) * (1.0 / D_SMLP) + EPS)
    for grp in range(N_GROUPS):
        sl = slice(grp * GROUP, (grp + 1) * GROUP)
        vc = z_ref[:, v0 + grp * GROUP:v0 + (grp + 1) * GROUP] - mean
        vn = vc * rstd * lg_ref[:, sl] + lb_ref[:, sl]

        @pl.when(is_sample)
        def _():
            vn_ref[:, sl] = vn

        w = jnp.where(causal, w_ref[grp], 0.0).astype(bf16)
        s = _dot(w, vn.astype(bf16)) + sb_ref[:, sl]
        y_b = z_ref[:, u0 + grp * GROUP:u0 + (grp + 1) * GROUP] * s
        y_ref[:, D_CONV + grp * GROUP:D_CONV + (grp + 1) * GROUP] = (
            group_rms(y_b, gb_ref[:, sl]).astype(y_ref.dtype))


def mixers(z, state, conv_w, conv_b, ln_g, ln_b, smlp_w, smlp_b, g_a, g_b):
    half = DEC_SEQ
    w_modes = jnp.stack([smlp_w, jnp.tile(smlp_w[:, :half, :half], (1, 2, 2))])
    b_rows = jnp.stack([smlp_b.T, jnp.tile(smlp_b[:, :half], (1, 2)).T])
    b_modes = jnp.repeat(b_rows, GROUP, axis=2)
    st = state.reshape(N_SAMPLE_BLOCKS, 4, D_CONV)
    vec = lambda a: a.reshape(1, -1)
    vspec = pl.BlockSpec((1, D_CONV), lambda i: (0, 0))
    mode = lambda i: i // N_PROMPT_BLOCKS
    sblk = lambda i: jnp.maximum(i - N_PROMPT_BLOCKS, 0)
    return pl.pallas_call(
        _mixer_kernel,
        grid=(N_BLOCKS,),
        in_specs=[pl.BlockSpec((ROWS, D_IN_PROJ), lambda i: (i, 0)),
                  pl.BlockSpec((1, 4, D_CONV), lambda i: (sblk(i), 0, 0)),
                  pl.BlockSpec((3, D_CONV), lambda i: (0, 0)),
                  vspec, vspec, vspec,
                  pl.BlockSpec((None, N_GROUPS, ROWS, ROWS), lambda i: (mode(i), 0, 0, 0)),
                  pl.BlockSpec((None, ROWS, D_SMLP), lambda i: (mode(i), 0, 0)),
                  vspec, vspec],
        out_specs=[pl.BlockSpec((ROWS, 2 * D_CONV), lambda i: (i, 0)),
                   pl.BlockSpec((ROWS, D_SMLP), lambda i: (sblk(i), 0)),
                   pl.BlockSpec((16, D_CONV), lambda i: (i, 0))],
        out_shape=[jax.ShapeDtypeStruct((N_TOK, 2 * D_CONV), bf16),
                   jax.ShapeDtypeStruct((N_SAMPLE, D_SMLP), f32),
                   jax.ShapeDtypeStruct((N_BLOCKS * 16, D_CONV), f32)],
        scratch_shapes=[pltpu.VMEM((8, D_CONV), f32)],
        compiler_params=_params(1), name="mixers",
    )(z, st, conv_w, vec(conv_b), vec(ln_g), vec(ln_b), w_modes, b_modes, vec(g_a), vec(g_b))


def _swiglu(a, wg, wu):
    g = _dot(a, wg.astype(bf16))
    u = _dot(a, wu.astype(bf16))
    return (g * jax.nn.sigmoid(g) * u).astype(bf16)


def _gate_up_kernel(a_ref, wg_ref, wu_ref, h_ref):
    h_ref[...] = _swiglu(a_ref[...], wg_ref[...], wu_ref[...])


def dense_gate_up(a, wg_all, wu_all, j):
    n = a.shape[0]
    wspec = pl.BlockSpec((None, D_MODEL, TF), lambda i, f: (j, 0, f))
    return pl.pallas_call(
        _gate_up_kernel,
        grid=(n // TM, D_FF // TF),
        in_specs=[pl.BlockSpec((TM, D_MODEL), lambda i, f: (i, 0)), wspec, wspec],
        out_specs=pl.BlockSpec((TM, TF), lambda i, f: (i, f)),
        out_shape=jax.ShapeDtypeStruct((n, D_FF), bf16),
        compiler_params=_params(2), name="dense_gate_up",
    )(a, wg_all, wu_all)


def _down_accumulate(h_ref, w_ref, o_ref, first):
    h = h_ref[...]
    for c in range(D_MODEL // TN_PROJ):
        cs = slice(c * TN_PROJ, (c + 1) * TN_PROJ)
        part = _dot(h, w_ref[:, cs].astype(bf16))
        if first:
            o_ref[:, cs] = part
        else:
            o_ref[:, cs] += part


def _down_kernel(h_ref, w_ref, o_ref):
    @pl.when(pl.program_id(1) == 0)
    def _():
        _down_accumulate(h_ref, w_ref, o_ref, True)

    @pl.when(pl.program_id(1) > 0)
    def _():
        _down_accumulate(h_ref, w_ref, o_ref, False)


def dense_down(h, wd_all, j):
    n = h.shape[0]
    return pl.pallas_call(
        _down_kernel,
        grid=(n // TM, D_FF // TF),
        in_specs=[pl.BlockSpec((TM, TF), lambda i, k: (i, k)),
                  pl.BlockSpec((None, TF, D_MODEL), lambda i, k: (j, k, 0))],
        out_specs=pl.BlockSpec((TM, D_MODEL), lambda i, k: (i, 0)),
        out_shape=jax.ShapeDtypeStruct((n, D_MODEL), f32),
        compiler_params=_params(2), name="dense_down",
    )(h, wd_all)


def _router_kernel(x_ref, g_ref, w_ref, b_ref, o_ref):
    xn = _rms(x_ref[...], g_ref[...]).astype(bf16)
    logits = _dot(xn, w_ref[...].astype(bf16)) + b_ref[...]
    lane = lax.broadcasted_iota(jnp.int32, logits.shape, 1)
    neg = jnp.float32(-jnp.inf)
    logits = jnp.where(lane < N_EXPERTS, logits, neg)
    m1 = jnp.max(logits, axis=-1, keepdims=True)
    i1 = jnp.min(jnp.where(logits == m1, lane, GROUP), axis=-1, keepdims=True)
    rest = jnp.where(lane == i1, neg, logits)
    m2 = jnp.max(rest, axis=-1, keepdims=True)
    i2 = jnp.min(jnp.where(rest == m2, lane, GROUP), axis=-1, keepdims=True)
    e = jnp.exp(m2 - m1)
    w1 = 1.0 / (1.0 + e)
    w2 = e / (1.0 + e)
    out = jnp.where(lane == 0, i1.astype(f32),
                    jnp.where(lane == 1, i2.astype(f32),
                              jnp.where(lane == 2, w1, jnp.where(lane == 3, w2, 0.0))))
    o_ref[...] = out


def router(x, g, w_router, b_router):
    n = x.shape[0]
    w_pad = jnp.pad(w_router, ((0, 0), (0, GROUP - N_EXPERTS)))
    b_pad = jnp.pad(b_router, (0, GROUP - N_EXPERTS)).reshape(1, GROUP)
    return pl.pallas_call(
        _router_kernel,
        grid=(n // TR,),
        in_specs=[pl.BlockSpec((TR, D_MODEL), lambda i: (i, 0)),
                  pl.BlockSpec((1, D_MODEL), lambda i: (0, 0)),
                  pl.BlockSpec((D_MODEL, GROUP), lambda i: (0, 0)),
                  pl.BlockSpec((1, GROUP), lambda i: (0, 0))],
        out_specs=pl.BlockSpec((TR, GROUP), lambda i: (i, 0)),
        out_shape=jax.ShapeDtypeStruct((n, GROUP), f32),
        compiler_params=_params(1), name="router",
    )(x, g.reshape(1, D_MODEL), w_pad, b_pad)


def _gather_rows(idx_ref, base, n_rows, src_hbm, dst_ref, sem):
    def issue(r, carry):
        pltpu.make_async_copy(src_hbm.at[pl.ds(idx_ref[base + r], 1)],
                              dst_ref.at[pl.ds(r, 1)], sem).start()
        return carry
    lax.fori_loop(0, n_rows, issue, 0)
    pltpu.make_async_copy(src_hbm.at[pl.ds(0, n_rows)], dst_ref.at[pl.ds(0, n_rows)], sem).wait()


def _dispatch_kernel(tok_ref, x_hbm, g_ref, o_ref, buf_ref, sem):
    _gather_rows(tok_ref, pl.program_id(0) * TG, TG, x_hbm, buf_ref, sem)
    o_ref[...] = _rms(buf_ref[...], g_ref[...]).astype(o_ref.dtype)


def dispatch(x, g, token_of_row):
    return pl.pallas_call(
        _dispatch_kernel,
        grid_spec=pltpu.PrefetchScalarGridSpec(
            num_scalar_prefetch=1,
            grid=(P_ROWS // TG,),
            in_specs=[pl.BlockSpec(memory_space=pl.ANY),
                      pl.BlockSpec((1, D_MODEL), lambda i, tok: (0, 0))],
            out_specs=pl.BlockSpec((TG, D_MODEL), lambda i, tok: (i, 0)),
            scratch_shapes=[pltpu.VMEM((TG, D_MODEL), f32), pltpu.SemaphoreType.DMA(())]),
        out_shape=jax.ShapeDtypeStruct((P_ROWS, D_MODEL), bf16),
        compiler_params=_params(1), name="moe_dispatch",
    )(token_of_row, x, g.reshape(1, D_MODEL))


def _moe_gate_up_kernel(be_ref, nb_ref, a_ref, wg_ref, wu_ref, h_ref):
    used = pl.program_id(0) < nb_ref[0]

    @pl.when(used)
    def _():
        h_ref[...] = _swiglu(a_ref[...], wg_ref[...], wu_ref[...])

    @pl.when(jnp.logical_not(used))
    def _():
        h_ref[...] = jnp.zeros_like(h_ref)


def moe_gate_up(a, wg_all, wu_all, j, block_expert, n_used):
    last = lambda nb: jnp.maximum(nb[0] - 1, 0)
    wspec = pl.BlockSpec((None, None, D_MODEL, TF), lambda b, f, be, nb: (j, be[b], 0, f))
    return pl.pallas_call(
        _moe_gate_up_kernel,
        grid_spec=pltpu.PrefetchScalarGridSpec(
            num_scalar_prefetch=2,
            grid=(N_MOE_BLOCKS, D_FF // TF),
            in_specs=[pl.BlockSpec((TM, D_MODEL), lambda b, f, be, nb: (jnp.minimum(b, last(nb)), 0)),
                      wspec, wspec],
            out_specs=pl.BlockSpec((TM, TF), lambda b, f, be, nb: (b, f))),
        out_shape=jax.ShapeDtypeStruct((P_ROWS, D_FF), bf16),
        compiler_params=_params(2), name="moe_gate_up",
    )(block_expert, n_used, a, wg_all, wu_all)


def _moe_down_kernel(be_ref, nb_ref, h_ref, w_ref, o_ref):
    used = pl.program_id(0) < nb_ref[0]
    k = pl.program_id(1)

    @pl.when(jnp.logical_and(used, k == 0))
    def _():
        _down_accumulate(h_ref, w_ref, o_ref, True)

    @pl.when(jnp.logical_and(used, k > 0))
    def _():
        _down_accumulate(h_ref, w_ref, o_ref, False)

    @pl.when(jnp.logical_and(jnp.logical_not(used), k == 0))
    def _():
        o_ref[...] = jnp.zeros_like(o_ref)


def moe_down(h, wd_all, j, block_expert, n_used):
    last = lambda nb: jnp.maximum(nb[0] - 1, 0)
    return pl.pallas_call(
        _moe_down_kernel,
        grid_spec=pltpu.PrefetchScalarGridSpec(
            num_scalar_prefetch=2,
            grid=(N_MOE_BLOCKS, D_FF // TF),
            in_specs=[pl.BlockSpec((TM, TF), lambda b, k, be, nb: (jnp.minimum(b, last(nb)), k)),
                      pl.BlockSpec((None, None, TF, D_MODEL), lambda b, k, be, nb: (j, be[b], k, 0))],
            out_specs=pl.BlockSpec((TM, D_MODEL), lambda b, k, be, nb: (b, 0))),
        out_shape=jax.ShapeDtypeStruct((P_ROWS, D_MODEL), f32),
        compiler_params=_params(2), name="moe_down",
    )(block_expert, n_used, h, wd_all)


def _combine_kernel(p1_ref, p2_ref, x_ref, r_ref, y_hbm, g_ref, o_ref, buf1_ref, buf2_ref, sem):
    base = pl.program_id(0) * TG
    _gather_rows(p1_ref, base, TG, y_hbm, buf1_ref, sem)
    _gather_rows(p2_ref, base, TG, y_hbm, buf2_ref, sem)
    w1 = r_ref[:, 2:3]
    w2 = r_ref[:, 3:4]
    x = x_ref[...] + (w1 * buf1_ref[...] + w2 * buf2_ref[...])
    o_ref[...] = _rms(x, g_ref[...])


def combine_final_norm(x, routing, y_rows, pos1, pos2, g):
    n = x.shape[0]
    return pl.pallas_call(
        _combine_kernel,
        grid_spec=pltpu.PrefetchScalarGridSpec(
            num_scalar_prefetch=2,
            grid=(n // TG,),
            in_specs=[pl.BlockSpec((TG, D_MODEL), lambda i, p1, p2: (i, 0)),
                      pl.BlockSpec((TG, GROUP), lambda i, p1, p2: (i, 0)),
                      pl.BlockSpec(memory_space=pl.ANY),
                      pl.BlockSpec((1, D_MODEL), lambda i, p1, p2: (0, 0))],
            out_specs=pl.BlockSpec((TG, D_MODEL), lambda i, p1, p2: (i, 0)),
            scratch_shapes=[pltpu.VMEM((TG, D_MODEL), f32), pltpu.VMEM((TG, D_MODEL), f32),
                            pltpu.SemaphoreType.DMA(())]),
        out_shape=jax.ShapeDtypeStruct((n, D_MODEL), f32),
        compiler_params=_params(1), name="moe_combine",
    )(pos1, pos2, x, routing, y_rows, g.reshape(1, D_MODEL))


def _routing_tables(routing):
    experts = routing[:, :TOP_K].astype(jnp.int32).reshape(-1)
    onehot = (experts[:, None] == jnp.arange(N_EXPERTS, dtype=jnp.int32)[None, :]).astype(jnp.int32)
    csum = jnp.cumsum(onehot, axis=0)
    rank = jnp.sum(onehot * (csum - 1), axis=1)
    counts = csum[-1]
    blocks = (counts + TM - 1) // TM
    block_end = jnp.cumsum(blocks)
    starts = (block_end - blocks) * TM
    pos = (starts[experts] + rank).astype(jnp.int32)
    n_used = block_end[-1:].astype(jnp.int32)
    blk = jnp.minimum(jnp.arange(N_MOE_BLOCKS, dtype=jnp.int32), n_used[0] - 1)
    block_expert = jnp.minimum(jnp.sum((blk[:, None] >= block_end[None, :]).astype(jnp.int32), axis=1),
                               N_EXPERTS - 1).astype(jnp.int32)
    token = jnp.arange(TOP_K * N_TOK, dtype=jnp.int32) // TOP_K
    token_of_row = jnp.zeros((P_ROWS,), jnp.int32).at[pos].set(token)
    pos = pos.reshape(N_TOK, TOP_K)
    return token_of_row, pos[:, 0], pos[:, 1], block_expert, n_used


def kernel(x_prompt, x_sample, state_conv, ln1_g, w_in, conv_w, conv_b, v_ln_g, v_ln_b, smlp_w, smlp_b, out_norm_a_g, out_norm_b_g, w_out, ln2_g, ffn_w_gate, ffn_w_up, ffn_w_down, router_w, router_b, moe_w_gate, moe_w_up, moe_w_down, final_norm_g):
    assert DEPTH == 2
    x = jnp.concatenate([x_prompt.reshape(N_PROMPT, D_MODEL), x_sample.reshape(N_SAMPLE, D_MODEL)])
    xn = rms_norm(x, ln1_g[0], bf16)
    conv_p, conv_s, v_rows = [], [], []
    y = None
    for l in range(DEPTH):
        z = in_proj(xn, w_in, l)
        ymix, vn, tails = mixers(z, state_conv[l], conv_w[l], conv_b[l], v_ln_g[l], v_ln_b[l],
                                 smlp_w[l], smlp_b[l], out_norm_a_g[l], out_norm_b_g[l])
        x = out_proj_residual(ymix, w_out, l, x)
        conv_p.append(jnp.stack([tails[(BLOCKS_PER_SEQ * (b + 1) - 1) * 16 + 14:BLOCKS_PER_SEQ * (b + 1) * 16]
                                 for b in range(BATCH)]))
        conv_s.append(tails[N_PROMPT_BLOCKS * 16:].reshape(DEC_BATCH, 8, D_CONV)[:, 6:8])
        v_rows.append(vn.reshape(DEC_BATCH, DEC_SEQ, D_SMLP))
        j = l // 2
        if l % 2 == 0:
            xn2 = rms_norm(x, ln2_g[l], bf16)
            h = dense_gate_up(xn2, ffn_w_gate, ffn_w_up, j)
            d = dense_down(h, ffn_w_down, j)
            x, xn = add_rms_norm(x, d, ln1_g[l + 1])
        else:
            routing = router(x, ln2_g[l], router_w[j], router_b[j])
            token_of_row, pos1, pos2, block_expert, n_used = _routing_tables(routing)
            rows = dispatch(x, ln2_g[l], token_of_row)
            h = moe_gate_up(rows, moe_w_gate, moe_w_up, j, block_expert, n_used)
            y_rows = moe_down(h, moe_w_down, j, block_expert, n_used)
            y = combine_final_norm(x, routing, y_rows, pos1, pos2, final_norm_g)
    return (y[:N_PROMPT].reshape(BATCH, SEQ, D_MODEL),
            y[N_PROMPT:].reshape(DEC_BATCH, DEC_SEQ, D_MODEL),
            jnp.stack(conv_p), jnp.stack(conv_s), jnp.stack(v_rows))
```

```python
import jax
import jax.numpy as jnp
from jax import lax
from jax.experimental import pallas as pl
from jax.experimental.pallas import tpu as pltpu

D_MODEL = 4096
BATCH = 2
SEQ = 4096
DEPTH = 2
DEC_BATCH = 32
DEC_SEQ = 64
D_CONV = 2048
D_SMLP = 2048
N_GROUPS = 16
GROUP = 128
SMLP_CHUNK = 128
D_IN_PROJ = 3 * D_CONV + 2 * D_SMLP
D_FF = 14336
N_EXPERTS = 8
TOP_K = 2
EPS = 1e-5

N_PROMPT = BATCH * SEQ
N_SAMPLE = DEC_BATCH * DEC_SEQ
N_TOK = N_PROMPT + N_SAMPLE
ROWS = SMLP_CHUNK
N_PROMPT_BLOCKS = N_PROMPT // ROWS
N_SAMPLE_BLOCKS = N_SAMPLE // ROWS
N_BLOCKS = N_PROMPT_BLOCKS + N_SAMPLE_BLOCKS
BLOCKS_PER_SEQ = SEQ // ROWS

V7X_VMEM_LIMIT_BYTES = 60 * 1024 * 1024

TM = 1024
TN_PROJ = 512
TF = 512
TF_SLAB = 256
TK = 512
TR = 256
TG = 256
P_ROWS = TOP_K * N_TOK + N_EXPERTS * TM
N_MOE_BLOCKS = P_ROWS // TM
PIECES = TM // TG

bf16 = jnp.bfloat16
f32 = jnp.float32


def _params(n_axes):
    return pltpu.CompilerParams(dimension_semantics=("arbitrary",) * n_axes,
                                vmem_limit_bytes=V7X_VMEM_LIMIT_BYTES)


def _rms(x, g):
    ms = jnp.mean(x * x, axis=-1, keepdims=True)
    return x * lax.rsqrt(ms + EPS) * g


def _dot(a, w):
    return jnp.dot(a, w, preferred_element_type=f32)


def _rms_kernel(x_ref, g_ref, o_ref):
    o_ref[...] = _rms(x_ref[...], g_ref[...]).astype(o_ref.dtype)


def rms_norm(x, g, out_dtype):
    n = x.shape[0]
    return pl.pallas_call(
        _rms_kernel,
        grid=(n // TR,),
        in_specs=[pl.BlockSpec((TR, D_MODEL), lambda i: (i, 0)),
                  pl.BlockSpec((1, D_MODEL), lambda i: (0, 0))],
        out_specs=pl.BlockSpec((TR, D_MODEL), lambda i: (i, 0)),
        out_shape=jax.ShapeDtypeStruct((n, D_MODEL), out_dtype),
        compiler_params=_params(1), name="rms_norm",
    )(x, g.reshape(1, D_MODEL))


def _split_specs(rows, width):
    n_p = N_PROMPT // rows
    col = (lambda i, j: j) if width != D_MODEL else (lambda i, j: 0)
    prompt = lambda i, j=0: (jnp.minimum(i, n_p - 1), jnp.where(i < n_p, col(i, j), 0))
    sample = lambda i, j=0: (jnp.maximum(i - n_p, 0), jnp.where(i < n_p, 0, col(i, j)))
    return pl.BlockSpec((rows, width), prompt), pl.BlockSpec((rows, width), sample)


def _rms_split_kernel(xp_ref, xs_ref, g_ref, o_ref):
    is_prompt = pl.program_id(0) < N_PROMPT // TR

    @pl.when(is_prompt)
    def _():
        o_ref[...] = _rms(xp_ref[...], g_ref[...]).astype(o_ref.dtype)

    @pl.when(jnp.logical_not(is_prompt))
    def _():
        o_ref[...] = _rms(xs_ref[...], g_ref[...]).astype(o_ref.dtype)


def rms_norm_split(x_p, x_s, g):
    p_spec, s_spec = _split_specs(TR, D_MODEL)
    return pl.pallas_call(
        _rms_split_kernel,
        grid=(N_TOK // TR,),
        in_specs=[p_spec, s_spec, pl.BlockSpec((1, D_MODEL), lambda i: (0, 0))],
        out_specs=pl.BlockSpec((TR, D_MODEL), lambda i: (i, 0)),
        out_shape=jax.ShapeDtypeStruct((N_TOK, D_MODEL), bf16),
        compiler_params=_params(1), name="rms_norm_split",
    )(x_p, x_s, g.reshape(1, D_MODEL))


def _add_rms_kernel(x_ref, d_ref, g_ref, xo_ref, no_ref):
    x = x_ref[...] + d_ref[...]
    xo_ref[...] = x
    no_ref[...] = _rms(x, g_ref[...]).astype(no_ref.dtype)


def add_rms_norm(x, d, g):
    n = x.shape[0]
    row = pl.BlockSpec((TR, D_MODEL), lambda i: (i, 0))
    return pl.pallas_call(
        _add_rms_kernel,
        grid=(n // TR,),
        in_specs=[row, row, pl.BlockSpec((1, D_MODEL), lambda i: (0, 0))],
        out_specs=[row, row],
        out_shape=[jax.ShapeDtypeStruct((n, D_MODEL), f32),
                   jax.ShapeDtypeStruct((n, D_MODEL), bf16)],
        compiler_params=_params(1), name="add_rms_norm",
    )(x, d, g.reshape(1, D_MODEL))


def _proj_kernel(a_ref, w_ref, o_ref):
    o_ref[...] = _dot(a_ref[...], w_ref[...].astype(bf16)).astype(o_ref.dtype)


def in_proj(a, w_all, layer):
    n = a.shape[0]
    return pl.pallas_call(
        _proj_kernel,
        grid=(n // TM, D_IN_PROJ // TN_PROJ),
        in_specs=[pl.BlockSpec((TM, D_MODEL), lambda i, j: (i, 0)),
                  pl.BlockSpec((None, D_MODEL, TN_PROJ), lambda i, j: (layer, 0, j))],
        out_specs=pl.BlockSpec((TM, TN_PROJ), lambda i, j: (i, j)),
        out_shape=jax.ShapeDtypeStruct((n, D_IN_PROJ), f32),
        compiler_params=_params(2), name="in_proj",
    )(a, w_all)


def _proj_res_kernel(a_ref, w_ref, x_ref, o_ref):
    o_ref[...] = x_ref[...] + _dot(a_ref[...], w_ref[...].astype(bf16))


def out_proj_residual(a, w_all, layer, x):
    n = a.shape[0]
    tile = pl.BlockSpec((TM, TN_PROJ), lambda i, j: (i, j))
    return pl.pallas_call(
        _proj_res_kernel,
        grid=(n // TM, D_MODEL // TN_PROJ),
        in_specs=[pl.BlockSpec((TM, D_MODEL), lambda i, j: (i, 0)),
                  pl.BlockSpec((None, D_MODEL, TN_PROJ), lambda i, j: (layer, 0, j)),
                  tile],
        out_specs=tile,
        out_shape=jax.ShapeDtypeStruct((n, D_MODEL), f32),
        compiler_params=_params(2), name="out_proj",
    )(a, w_all, x)


def _proj_res_split_kernel(a_ref, w_ref, xp_ref, xs_ref, o_ref):
    is_prompt = pl.program_id(0) < N_PROMPT // TM
    y = _dot(a_ref[...], w_ref[...].astype(bf16))

    @pl.when(is_prompt)
    def _():
        o_ref[...] = xp_ref[...] + y

    @pl.when(jnp.logical_not(is_prompt))
    def _():
        o_ref[...] = xs_ref[...] + y


def out_proj_residual_split(a, w_all, layer, x_p, x_s):
    p_spec, s_spec = _split_specs(TM, TN_PROJ)
    return pl.pallas_call(
        _proj_res_split_kernel,
        grid=(N_TOK // TM, D_MODEL // TN_PROJ),
        in_specs=[pl.BlockSpec((TM, D_MODEL), lambda i, j: (i, 0)),
                  pl.BlockSpec((None, D_MODEL, TN_PROJ), lambda i, j: (layer, 0, j)),
                  p_spec, s_spec],
        out_specs=pl.BlockSpec((TM, TN_PROJ), lambda i, j: (i, j)),
        out_shape=jax.ShapeDtypeStruct((N_TOK, D_MODEL), f32),
        compiler_params=_params(2), name="out_proj_split",
    )(a, w_all, x_p, x_s)


def _mixer_kernel(z_ref, st_ref, cw_ref, cb_ref, lg_ref, lb_ref, w_ref, sb_ref, ga_ref, gb_ref,
                  y_ref, vn_ref, tail_ref, prev_ref):
    i = pl.program_id(0)

    @pl.when(i == 0)
    def _():
        prev_ref[...] = jnp.zeros_like(prev_ref)

    is_sample = i >= N_PROMPT_BLOCKS
    sample_row = jnp.full((1, GROUP), i, jnp.int32) >= N_PROMPT_BLOCKS
    seq_start_row = (jnp.full((1, GROUP), i, jnp.int32) % BLOCKS_PER_SEQ) == 0
    mid = jnp.where(is_sample, DEC_SEQ, -8)
    row = lax.broadcasted_iota(jnp.int32, (ROWS, GROUP), 0)
    col = lax.broadcasted_iota(jnp.int32, (ROWS, GROUP), 1)
    sample_full = jnp.full((ROWS, GROUP), i, jnp.int32) >= N_PROMPT_BLOCKS
    same_seq = jnp.logical_or(jnp.logical_not(sample_full), (row // DEC_SEQ) == (col // DEC_SEQ))
    causal = jnp.logical_and(col <= row, same_seq)

    def group_rms(y, g):
        ms = jnp.mean(y * y, axis=-1, keepdims=True)
        return y * lax.rsqrt(ms + EPS) * g

    for grp in range(N_GROUPS):
        sl = slice(grp * GROUP, (grp + 1) * GROUP)
        gate_b = z_ref[:, grp * GROUP:(grp + 1) * GROUP]
        cin = (z_ref[:, D_CONV + grp * GROUP:D_CONV + (grp + 1) * GROUP]
               * z_ref[:, 2 * D_CONV + grp * GROUP:2 * D_CONV + (grp + 1) * GROUP])
        zero = jnp.zeros((1, GROUP), f32)
        a1 = jnp.where(sample_row, st_ref[0, 1:2, sl], jnp.where(seq_start_row, zero, prev_ref[7:8, sl]))
        a2 = jnp.where(sample_row, st_ref[0, 0:1, sl], jnp.where(seq_start_row, zero, prev_ref[6:7, sl]))
        b1 = st_ref[0, 3:4, sl]
        b2 = st_ref[0, 2:3, sl]
        s1 = pltpu.roll(cin, 1, 0)
        s2 = pltpu.roll(cin, 2, 0)
        s1 = jnp.where(row == 0, a1, s1)
        s1 = jnp.where(row == mid, b1, s1)
        s2 = jnp.where(row == 0, a2, jnp.where(row == 1, a1, s2))
        s2 = jnp.where(row == mid, b2, jnp.where(row == mid + 1, b1, s2))
        conv = cb_ref[:, sl] + cw_ref[0:1, sl] * s2 + cw_ref[1:2, sl] * s1 + cw_ref[2:3, sl] * cin
        y_ref[:, sl] = group_rms(gate_b * conv, ga_ref[:, sl]).astype(y_ref.dtype)
        tail_ref[0:8, sl] = cin[DEC_SEQ - 8:DEC_SEQ]
        tail_ref[8:16, sl] = cin[ROWS - 8:ROWS]
        prev_ref[:, sl] = cin[ROWS - 8:ROWS]

    v0 = 3 * D_CONV + D_SMLP
    u0 = 3 * D_CONV
    acc = z_ref[:, v0:v0 + GROUP]
    for grp in range(1, N_GROUPS):
        acc = acc + z_ref[:, v0 + grp * GROUP:v0 + (grp + 1) * GROUP]
    mean = jnp.sum(acc, axis=-1, keepdims=True) * (1.0 / D_SMLP)
    acc = jnp.zeros((ROWS, GROUP), f32)
    for grp in range(N_GROUPS):
        vc = z_ref[:, v0 + grp * GROUP:v0 + (grp + 1) * GROUP] - mean
        acc = acc + vc * vc
    rstd = lax.rsqrt(jnp.sum(acc, axis=-1, keepdims=True) * (1.0 / D_SMLP) + EPS)
    for grp in range(N_GROUPS):
        sl = slice(grp * GROUP, (grp + 1) * GROUP)
        vc = z_ref[:, v0 + grp * GROUP:v0 + (grp + 1) * GROUP] - mean
        vn = vc * rstd * lg_ref[:, sl] + lb_ref[:, sl]

        vn_ref[:, sl] = vn
        w = jnp.where(causal, w_ref[grp], 0.0).astype(bf16)
        s = _dot(w, vn.astype(bf16)) + sb_ref[:, sl]
        y_b = z_ref[:, u0 + grp * GROUP:u0 + (grp + 1) * GROUP] * s
        y_ref[:, D_CONV + grp * GROUP:D_CONV + (grp + 1) * GROUP] = (
            group_rms(y_b, gb_ref[:, sl]).astype(y_ref.dtype))


def mixers(z, state, conv_w, conv_b, ln_g, ln_b, smlp_w, smlp_b, g_a, g_b):
    half = DEC_SEQ
    w_modes = jnp.stack([smlp_w, jnp.tile(smlp_w[:, :half, :half], (1, 2, 2))])
    b_rows = jnp.stack([smlp_b.T, jnp.tile(smlp_b[:, :half], (1, 2)).T])
    b_modes = jnp.repeat(b_rows, GROUP, axis=2)
    st = state.reshape(N_SAMPLE_BLOCKS, 4, D_CONV)
    vec = lambda a: a.reshape(1, -1)
    vspec = pl.BlockSpec((1, D_CONV), lambda i: (0, 0))
    mode = lambda i: i // N_PROMPT_BLOCKS
    sblk = lambda i: jnp.maximum(i - N_PROMPT_BLOCKS, 0)
    return pl.pallas_call(
        _mixer_kernel,
        grid=(N_BLOCKS,),
        in_specs=[pl.BlockSpec((ROWS, D_IN_PROJ), lambda i: (i, 0)),
                  pl.BlockSpec((1, 4, D_CONV), lambda i: (sblk(i), 0, 0)),
                  pl.BlockSpec((3, D_CONV), lambda i: (0, 0)),
                  vspec, vspec, vspec,
                  pl.BlockSpec((None, N_GROUPS, ROWS, ROWS), lambda i: (mode(i), 0, 0, 0)),
                  pl.BlockSpec((None, ROWS, D_SMLP), lambda i: (mode(i), 0, 0)),
                  vspec, vspec],
        out_specs=[pl.BlockSpec((ROWS, 2 * D_CONV), lambda i: (i, 0)),
                   pl.BlockSpec((ROWS, D_SMLP), lambda i: (sblk(i), 0)),
                   pl.BlockSpec((16, D_CONV), lambda i: (i, 0))],
        out_shape=[jax.ShapeDtypeStruct((N_TOK, 2 * D_CONV), bf16),
                   jax.ShapeDtypeStruct((N_SAMPLE, D_SMLP), f32),
                   jax.ShapeDtypeStruct((N_BLOCKS * 16, D_CONV), f32)],
        scratch_shapes=[pltpu.VMEM((8, D_CONV), f32)],
        compiler_params=_params(1), name="mixers",
    )(z, st, conv_w, vec(conv_b), vec(ln_g), vec(ln_b), w_modes, b_modes, vec(g_a), vec(g_b))


def _swiglu_rows(a_ref, wg_ref, wu_ref, h_ref, rows):
    a = a_ref[0:rows, :]
    for c in range(TF // TF_SLAB):
        cs = slice(c * TF_SLAB, (c + 1) * TF_SLAB)
        g = _dot(a, wg_ref[:, cs].astype(bf16))
        u = _dot(a, wu_ref[:, cs].astype(bf16))
        h_ref[0:rows, cs] = (g * jax.nn.sigmoid(g) * u).astype(h_ref.dtype)


def _gate_up_kernel(a_ref, wg_ref, wu_ref, h_ref):
    _swiglu_rows(a_ref, wg_ref, wu_ref, h_ref, TM)


def _resident_rows_spec(index_map):
    return pl.BlockSpec((TM, D_MODEL), index_map, pipeline_mode=pl.Buffered(1))


def dense_gate_up(a, wg_all, wu_all, j):
    n = a.shape[0]
    wspec = pl.BlockSpec((None, D_MODEL, TF), lambda i, f: (j, 0, f))
    return pl.pallas_call(
        _gate_up_kernel,
        grid=(n // TM, D_FF // TF),
        in_specs=[_resident_rows_spec(lambda i, f: (i, 0)), wspec, wspec],
        out_specs=pl.BlockSpec((TM, TF), lambda i, f: (i, f)),
        out_shape=jax.ShapeDtypeStruct((n, D_FF), bf16),
        compiler_params=_params(2), name="dense_gate_up",
    )(a, wg_all, wu_all)


def _down_accumulate(h_ref, w_ref, o_ref, first):
    h = h_ref[...]
    for c in range(D_MODEL // TN_PROJ):
        cs = slice(c * TN_PROJ, (c + 1) * TN_PROJ)
        part = _dot(h, w_ref[:, cs].astype(bf16))
        if first:
            o_ref[:, cs] = part
        else:
            o_ref[:, cs] += part


def _down_kernel(h_ref, w_ref, o_ref):
    @pl.when(pl.program_id(1) == 0)
    def _():
        _down_accumulate(h_ref, w_ref, o_ref, True)

    @pl.when(pl.program_id(1) > 0)
    def _():
        _down_accumulate(h_ref, w_ref, o_ref, False)


def dense_down(h, wd_all, j):
    n = h.shape[0]
    return pl.pallas_call(
        _down_kernel,
        grid=(n // TM, D_FF // TK),
        in_specs=[pl.BlockSpec((TM, TK), lambda i, k: (i, k)),
                  pl.BlockSpec((None, TK, D_MODEL), lambda i, k: (j, k, 0))],
        out_specs=pl.BlockSpec((TM, D_MODEL), lambda i, k: (i, 0)),
        out_shape=jax.ShapeDtypeStruct((n, D_MODEL), f32),
        compiler_params=_params(2), name="dense_down",
    )(h, wd_all)


def _router_kernel(x_ref, g_ref, w_ref, b_ref, o_ref):
    xn = _rms(x_ref[...], g_ref[...]).astype(bf16)
    logits = _dot(xn, w_ref[...].astype(bf16)) + b_ref[...]
    lane = lax.broadcasted_iota(jnp.int32, logits.shape, 1)
    neg = jnp.float32(-jnp.inf)
    logits = jnp.where(lane < N_EXPERTS, logits, neg)
    m1 = jnp.max(logits, axis=-1, keepdims=True)
    i1 = jnp.min(jnp.where(logits == m1, lane, GROUP), axis=-1, keepdims=True)
    rest = jnp.where(lane == i1, neg, logits)
    m2 = jnp.max(rest, axis=-1, keepdims=True)
    i2 = jnp.min(jnp.where(rest == m2, lane, GROUP), axis=-1, keepdims=True)
    e = jnp.exp(m2 - m1)
    w1 = 1.0 / (1.0 + e)
    w2 = e / (1.0 + e)
    out = jnp.where(lane == 0, i1.astype(f32),
                    jnp.where(lane == 1, i2.astype(f32),
                              jnp.where(lane == 2, w1, jnp.where(lane == 3, w2, 0.0))))
    o_ref[...] = out


def router(x, g, w_router, b_router):
    n = x.shape[0]
    w_pad = jnp.pad(w_router, ((0, 0), (0, GROUP - N_EXPERTS)))
    b_pad = jnp.pad(b_router, (0, GROUP - N_EXPERTS)).reshape(1, GROUP)
    return pl.pallas_call(
        _router_kernel,
        grid=(n // TR,),
        in_specs=[pl.BlockSpec((TR, D_MODEL), lambda i: (i, 0)),
                  pl.BlockSpec((1, D_MODEL), lambda i: (0, 0)),
                  pl.BlockSpec((D_MODEL, GROUP), lambda i: (0, 0)),
                  pl.BlockSpec((1, GROUP), lambda i: (0, 0))],
        out_specs=pl.BlockSpec((TR, GROUP), lambda i: (i, 0)),
        out_shape=jax.ShapeDtypeStruct((n, GROUP), f32),
        compiler_params=_params(1), name="router",
    )(x, g.reshape(1, D_MODEL), w_pad, b_pad)


def _start_row_gather(idx_ref, base, src_hbm, dst_ref, sem):
    def issue(r, carry):
        pltpu.make_async_copy(src_hbm.at[pl.ds(idx_ref[base + r], 1)],
                              dst_ref.at[pl.ds(r, 1)], sem).start()
        return carry
    lax.fori_loop(0, TG, issue, 0, unroll=8)


def _wait_row_gather(src_hbm, dst_ref, sem):
    pltpu.make_async_copy(src_hbm.at[pl.ds(0, TG)], dst_ref, sem).wait()


def _dispatch_kernel(tok_ref, np_ref, x_hbm, g_ref, o_ref, buf_ref, sem):
    i = pl.program_id(0)
    n = pl.num_programs(0)

    def has_rows(step):
        return (step % PIECES) < np_ref[step // PIECES]

    def start(step):
        slot = step % 2
        _start_row_gather(tok_ref, step * TG, x_hbm, buf_ref.at[slot], sem.at[slot])

    @pl.when(jnp.logical_and(i == 0, has_rows(0)))
    def _():
        start(0)

    nxt = jnp.minimum(i + 1, n - 1)

    @pl.when(jnp.logical_and(i + 1 < n, has_rows(nxt)))
    def _():
        start(nxt)

    @pl.when(has_rows(i))
    def _():
        slot = i % 2
        _wait_row_gather(x_hbm, buf_ref.at[slot], sem.at[slot])
        o_ref[...] = _rms(buf_ref[slot], g_ref[...]).astype(o_ref.dtype)

    @pl.when(jnp.logical_not(has_rows(i)))
    def _():
        o_ref[...] = jnp.zeros_like(o_ref)


def dispatch(x, g, token_of_row, n_pieces):
    return pl.pallas_call(
        _dispatch_kernel,
        grid_spec=pltpu.PrefetchScalarGridSpec(
            num_scalar_prefetch=2,
            grid=(P_ROWS // TG,),
            in_specs=[pl.BlockSpec(memory_space=pl.ANY),
                      pl.BlockSpec((1, D_MODEL), lambda i, tok, npc: (0, 0))],
            out_specs=pl.BlockSpec((TG, D_MODEL), lambda i, tok, npc: (i, 0)),
            scratch_shapes=[pltpu.VMEM((2, TG, D_MODEL), f32), pltpu.SemaphoreType.DMA((2,))]),
        out_shape=jax.ShapeDtypeStruct((P_ROWS, D_MODEL), bf16),
        compiler_params=_params(1), name="moe_dispatch",
    )(token_of_row, n_pieces, x, g.reshape(1, D_MODEL))


def _moe_gate_up_kernel(be_ref, np_ref, a_ref, wg_ref, wu_ref, h_ref):
    n_pieces = np_ref[pl.program_id(0)]
    for q in range(PIECES + 1):
        @pl.when(n_pieces == q)
        def _():
            rows = q * TG
            if rows > 0:
                _swiglu_rows(a_ref, wg_ref, wu_ref, h_ref, rows)
            if rows < TM:
                h_ref[rows:TM, :] = jnp.zeros((TM - rows, TF), h_ref.dtype)


def _last_used(np_ref):
    return lambda b: jnp.minimum(b, np_ref[N_MOE_BLOCKS] - 1)


def moe_gate_up(a, wg_all, wu_all, j, block_expert, n_pieces):
    wspec = pl.BlockSpec((None, None, D_MODEL, TF), lambda b, f, be, npc: (j, be[b], 0, f))
    return pl.pallas_call(
        _moe_gate_up_kernel,
        grid_spec=pltpu.PrefetchScalarGridSpec(
            num_scalar_prefetch=2,
            grid=(N_MOE_BLOCKS, D_FF // TF),
            in_specs=[_resident_rows_spec(lambda b, f, be, npc: (_last_used(npc)(b), 0)), wspec, wspec],
            out_specs=pl.BlockSpec((TM, TF), lambda b, f, be, npc: (b, f))),
        out_shape=jax.ShapeDtypeStruct((P_ROWS, D_FF), bf16),
        compiler_params=_params(2), name="moe_gate_up",
    )(block_expert, n_pieces, a, wg_all, wu_all)


def _down_accumulate_rows(h_ref, w_ref, o_ref, rows, first):
    if rows > 0:
        h = h_ref[0:rows, :]
        for c in range(D_MODEL // TN_PROJ):
            cs = slice(c * TN_PROJ, (c + 1) * TN_PROJ)
            part = _dot(h, w_ref[:, cs].astype(bf16))
            if first:
                o_ref[0:rows, cs] = part
            else:
                o_ref[0:rows, cs] += part
    if first and rows < TM:
        o_ref[rows:TM, :] = jnp.zeros((TM - rows, D_MODEL), o_ref.dtype)


def _moe_down_kernel(be_ref, np_ref, h_ref, w_ref, o_ref):
    n_pieces = np_ref[pl.program_id(0)]
    k = pl.program_id(1)
    for q in range(PIECES + 1):
        @pl.when(jnp.logical_and(n_pieces == q, k == 0))
        def _():
            _down_accumulate_rows(h_ref, w_ref, o_ref, q * TG, True)

        if q > 0:
            @pl.when(jnp.logical_and(n_pieces == q, k > 0))
            def _():
                _down_accumulate_rows(h_ref, w_ref, o_ref, q * TG, False)


def moe_down(h, wd_all, j, block_expert, n_pieces):
    return pl.pallas_call(
        _moe_down_kernel,
        grid_spec=pltpu.PrefetchScalarGridSpec(
            num_scalar_prefetch=2,
            grid=(N_MOE_BLOCKS, D_FF // TK),
            in_specs=[pl.BlockSpec((TM, TK), lambda b, k, be, npc: (_last_used(npc)(b), k)),
                      pl.BlockSpec((None, None, TK, D_MODEL), lambda b, k, be, npc: (j, be[b], k, 0))],
            out_specs=pl.BlockSpec((TM, D_MODEL), lambda b, k, be, npc: (b, 0))),
        out_shape=jax.ShapeDtypeStruct((P_ROWS, D_MODEL), f32),
        compiler_params=_params(2), name="moe_down",
    )(block_expert, n_pieces, h, wd_all)


def _combine_kernel(p1_ref, p2_ref, x_ref, r_ref, y_hbm, g_ref, op_ref, os_ref, buf_ref, sem):
    i = pl.program_id(0)
    n = pl.num_programs(0)

    def start(step):
        slot = step % 2
        _start_row_gather(p1_ref, step * TG, y_hbm, buf_ref.at[slot, 0], sem.at[slot])
        _start_row_gather(p2_ref, step * TG, y_hbm, buf_ref.at[slot, 1], sem.at[slot])

    @pl.when(i == 0)
    def _():
        start(0)

    @pl.when(i + 1 < n)
    def _():
        start(i + 1)

    slot = i % 2
    _wait_row_gather(y_hbm, buf_ref.at[slot, 0], sem.at[slot])
    _wait_row_gather(y_hbm, buf_ref.at[slot, 1], sem.at[slot])

    def result():
        x = x_ref[...] + (r_ref[:, 2:3] * buf_ref[slot, 0] + r_ref[:, 3:4] * buf_ref[slot, 1])
        return _rms(x, g_ref[...])

    is_prompt = i < N_PROMPT // TG

    @pl.when(is_prompt)
    def _():
        op_ref[...] = result()

    @pl.when(jnp.logical_not(is_prompt))
    def _():
        os_ref[...] = result()


def combine_final_norm(x, routing, y_rows, pos1, pos2, g):
    n_p = N_PROMPT // TG
    return pl.pallas_call(
        _combine_kernel,
        grid_spec=pltpu.PrefetchScalarGridSpec(
            num_scalar_prefetch=2,
            grid=(N_TOK // TG,),
            in_specs=[pl.BlockSpec((TG, D_MODEL), lambda i, p1, p2: (i, 0)),
                      pl.BlockSpec((TG, GROUP), lambda i, p1, p2: (i, 0)),
                      pl.BlockSpec(memory_space=pl.ANY),
                      pl.BlockSpec((1, D_MODEL), lambda i, p1, p2: (0, 0))],
            out_specs=[pl.BlockSpec((TG, D_MODEL), lambda i, p1, p2: (jnp.minimum(i, n_p - 1), 0)),
                       pl.BlockSpec((TG, D_MODEL), lambda i, p1, p2: (jnp.maximum(i - n_p, 0), 0))],
            scratch_shapes=[pltpu.VMEM((2, 2, TG, D_MODEL), f32), pltpu.SemaphoreType.DMA((2,))]),
        out_shape=[jax.ShapeDtypeStruct((N_PROMPT, D_MODEL), f32),
                   jax.ShapeDtypeStruct((N_SAMPLE, D_MODEL), f32)],
        compiler_params=_params(1), name="moe_combine",
    )(pos1, pos2, x, routing, y_rows, g.reshape(1, D_MODEL))


def _routing_tables(routing):
    experts = routing[:, :TOP_K].astype(jnp.int32).reshape(-1)
    onehot = (experts[:, None] == jnp.arange(N_EXPERTS, dtype=jnp.int32)[None, :]).astype(jnp.int32)
    csum = jnp.cumsum(onehot, axis=0)
    rank = jnp.sum(onehot * (csum - 1), axis=1)
    counts = csum[-1]
    blocks = (counts + TM - 1) // TM
    block_end = jnp.cumsum(blocks)
    block_start = block_end - blocks
    pos = (block_start[experts] * TM + rank).astype(jnp.int32)
    n_used = block_end[-1]
    blk = jnp.arange(N_MOE_BLOCKS, dtype=jnp.int32)
    block_expert = jnp.minimum(
        jnp.sum((jnp.minimum(blk, n_used - 1)[:, None] >= block_end[None, :]).astype(jnp.int32), axis=1),
        N_EXPERTS - 1).astype(jnp.int32)
    rows_in_block = jnp.clip(counts[block_expert] - (blk - block_start[block_expert]) * TM, 0, TM)
    rows_in_block = jnp.where(blk < n_used, rows_in_block, 0)
    n_pieces = jnp.concatenate([(rows_in_block + TG - 1) // TG, n_used[None]]).astype(jnp.int32)
    token = jnp.arange(TOP_K * N_TOK, dtype=jnp.int32) // TOP_K
    token_of_row = jnp.zeros((P_ROWS,), jnp.int32).at[pos].set(token)
    pos = pos.reshape(N_TOK, TOP_K)
    return token_of_row, pos[:, 0], pos[:, 1], block_expert, n_pieces


def kernel(x_prompt, x_sample, state_conv, ln1_g, w_in, conv_w, conv_b, v_ln_g, v_ln_b, smlp_w, smlp_b, out_norm_a_g, out_norm_b_g, w_out, ln2_g, ffn_w_gate, ffn_w_up, ffn_w_down, router_w, router_b, moe_w_gate, moe_w_up, moe_w_down, final_norm_g):
    assert DEPTH == 2
    x_p = x_prompt.reshape(N_PROMPT, D_MODEL)
    x_s = x_sample.reshape(N_SAMPLE, D_MODEL)
    xn = rms_norm_split(x_p, x_s, ln1_g[0])
    conv_p, conv_s, v_rows = [], [], []
    x = y_p = y_s = None
    for l in range(DEPTH):
        z = in_proj(xn, w_in, l)
        ymix, vn, tails = mixers(z, state_conv[l], conv_w[l], conv_b[l], v_ln_g[l], v_ln_b[l],
                                 smlp_w[l], smlp_b[l], out_norm_a_g[l], out_norm_b_g[l])
        if l == 0:
            x = out_proj_residual_split(ymix, w_out, l, x_p, x_s)
        else:
            x = out_proj_residual(ymix, w_out, l, x)
        conv_p.append(jnp.stack([tails[(BLOCKS_PER_SEQ * (b + 1) - 1) * 16 + 14:BLOCKS_PER_SEQ * (b + 1) * 16]
                                 for b in range(BATCH)]))
        conv_s.append(tails[N_PROMPT_BLOCKS * 16:].reshape(DEC_BATCH, 8, D_CONV)[:, 6:8])
        v_rows.append(vn.reshape(DEC_BATCH, DEC_SEQ, D_SMLP))
        j = l // 2
        if l % 2 == 0:
            xn2 = rms_norm(x, ln2_g[l], bf16)
            h = dense_gate_up(xn2, ffn_w_gate, ffn_w_up, j)
            d = dense_down(h, ffn_w_down, j)
            x, xn = add_rms_norm(x, d, ln1_g[l + 1])
        else:
            routing = router(x, ln2_g[l], router_w[j], router_b[j])
            token_of_row, pos1, pos2, block_expert, n_pieces = _routing_tables(routing)
            rows = dispatch(x, ln2_g[l], token_of_row, n_pieces)
            h = moe_gate_up(rows, moe_w_gate, moe_w_up, j, block_expert, n_pieces)
            y_rows = moe_down(h, moe_w_down, j, block_expert, n_pieces)
            y_p, y_s = combine_final_norm(x, routing, y_rows, pos1, pos2, final_norm_g)
    return (y_p.reshape(BATCH, SEQ, D_MODEL), y_s.reshape(DEC_BATCH, DEC_SEQ, D_MODEL),
            jnp.stack(conv_p), jnp.stack(conv_s), jnp.stack(v_rows))
```

```python
import jax
import jax.numpy as jnp
from jax import lax
from jax.experimental import pallas as pl
from jax.experimental.pallas import tpu as pltpu

D_MODEL = 4096
BATCH = 2
SEQ = 4096
DEPTH = 2
DEC_BATCH = 32
DEC_SEQ = 64
D_CONV = 2048
D_SMLP = 2048
N_GROUPS = 16
GROUP = 128
SMLP_CHUNK = 128
D_IN_PROJ = 3 * D_CONV + 2 * D_SMLP
D_FF = 14336
N_EXPERTS = 8
TOP_K = 2
EPS = 1e-5

N_PROMPT = BATCH * SEQ
N_SAMPLE = DEC_BATCH * DEC_SEQ
N_TOK = N_PROMPT + N_SAMPLE
ROWS = SMLP_CHUNK
N_PROMPT_BLOCKS = N_PROMPT // ROWS
N_SAMPLE_BLOCKS = N_SAMPLE // ROWS
N_BLOCKS = N_PROMPT_BLOCKS + N_SAMPLE_BLOCKS
BLOCKS_PER_SEQ = SEQ // ROWS

V7X_VMEM_LIMIT_BYTES = 60 * 1024 * 1024

TM = 1024
TN_PROJ = 512
TN_IN = 1024
TF = 512
TF_SLAB = 256
TK = 512
TR = 256
TG = 256
P_ROWS = TOP_K * N_TOK + N_EXPERTS * TM
N_MOE_BLOCKS = P_ROWS // TM
PIECES = TM // TG

bf16 = jnp.bfloat16
f32 = jnp.float32


def _params(n_axes):
    return pltpu.CompilerParams(dimension_semantics=("arbitrary",) * n_axes,
                                vmem_limit_bytes=V7X_VMEM_LIMIT_BYTES)


def _rms(x, g):
    ms = jnp.mean(x * x, axis=-1, keepdims=True)
    return x * lax.rsqrt(ms + EPS) * g


def _dot(a, w):
    return jnp.dot(a, w, preferred_element_type=f32)


def _rms_kernel(x_ref, g_ref, o_ref):
    o_ref[...] = _rms(x_ref[...], g_ref[...]).astype(o_ref.dtype)


def rms_norm(x, g, out_dtype):
    n = x.shape[0]
    return pl.pallas_call(
        _rms_kernel,
        grid=(n // TR,),
        in_specs=[pl.BlockSpec((TR, D_MODEL), lambda i: (i, 0)),
                  pl.BlockSpec((1, D_MODEL), lambda i: (0, 0))],
        out_specs=pl.BlockSpec((TR, D_MODEL), lambda i: (i, 0)),
        out_shape=jax.ShapeDtypeStruct((n, D_MODEL), out_dtype),
        compiler_params=_params(1), name="rms_norm",
    )(x, g.reshape(1, D_MODEL))


def _split_specs(rows, width):
    n_p = N_PROMPT // rows
    col = (lambda i, j: j) if width != D_MODEL else (lambda i, j: 0)
    prompt = lambda i, j=0: (jnp.minimum(i, n_p - 1), jnp.where(i < n_p, col(i, j), 0))
    sample = lambda i, j=0: (jnp.maximum(i - n_p, 0), jnp.where(i < n_p, 0, col(i, j)))
    return pl.BlockSpec((rows, width), prompt), pl.BlockSpec((rows, width), sample)


def _rms_split_kernel(xp_ref, xs_ref, g_ref, o_ref):
    is_prompt = pl.program_id(0) < N_PROMPT // TR

    @pl.when(is_prompt)
    def _():
        o_ref[...] = _rms(xp_ref[...], g_ref[...]).astype(o_ref.dtype)

    @pl.when(jnp.logical_not(is_prompt))
    def _():
        o_ref[...] = _rms(xs_ref[...], g_ref[...]).astype(o_ref.dtype)


def rms_norm_split(x_p, x_s, g):
    p_spec, s_spec = _split_specs(TR, D_MODEL)
    return pl.pallas_call(
        _rms_split_kernel,
        grid=(N_TOK // TR,),
        in_specs=[p_spec, s_spec, pl.BlockSpec((1, D_MODEL), lambda i: (0, 0))],
        out_specs=pl.BlockSpec((TR, D_MODEL), lambda i: (i, 0)),
        out_shape=jax.ShapeDtypeStruct((N_TOK, D_MODEL), bf16),
        compiler_params=_params(1), name="rms_norm_split",
    )(x_p, x_s, g.reshape(1, D_MODEL))


def _add_rms_kernel(x_ref, d_ref, g_ref, xo_ref, no_ref):
    x = x_ref[...] + d_ref[...]
    xo_ref[...] = x
    no_ref[...] = _rms(x, g_ref[...]).astype(no_ref.dtype)


def add_rms_norm(x, d, g):
    n = x.shape[0]
    row = pl.BlockSpec((TR, D_MODEL), lambda i: (i, 0))
    return pl.pallas_call(
        _add_rms_kernel,
        grid=(n // TR,),
        in_specs=[row, row, pl.BlockSpec((1, D_MODEL), lambda i: (0, 0))],
        out_specs=[row, row],
        out_shape=[jax.ShapeDtypeStruct((n, D_MODEL), f32),
                   jax.ShapeDtypeStruct((n, D_MODEL), bf16)],
        compiler_params=_params(1), name="add_rms_norm",
    )(x, d, g.reshape(1, D_MODEL))


def _proj_kernel(a_ref, w_ref, o_ref):
    a = a_ref[...]
    for c in range(TN_IN // TN_PROJ):
        cs = slice(c * TN_PROJ, (c + 1) * TN_PROJ)
        o_ref[:, cs] = _dot(a, w_ref[:, cs].astype(bf16))


def in_proj(a, w_all, layer):
    n = a.shape[0]
    return pl.pallas_call(
        _proj_kernel,
        grid=(n // TM, D_IN_PROJ // TN_IN),
        in_specs=[_resident_rows_spec(lambda i, j: (i, 0)),
                  pl.BlockSpec((None, D_MODEL, TN_IN), lambda i, j: (layer, 0, j))],
        out_specs=pl.BlockSpec((TM, TN_IN), lambda i, j: (i, j)),
        out_shape=jax.ShapeDtypeStruct((n, D_IN_PROJ), f32),
        compiler_params=_params(2), name="in_proj",
    )(a, w_all)


def _proj_res_kernel(a_ref, w_ref, x_ref, o_ref):
    o_ref[...] = x_ref[...] + _dot(a_ref[...], w_ref[...].astype(bf16))


def out_proj_residual(a, w_all, layer, x):
    n = a.shape[0]
    tile = pl.BlockSpec((TM, TN_PROJ), lambda i, j: (i, j))
    return pl.pallas_call(
        _proj_res_kernel,
        grid=(n // TM, D_MODEL // TN_PROJ),
        in_specs=[pl.BlockSpec((TM, D_MODEL), lambda i, j: (i, 0)),
                  pl.BlockSpec((None, D_MODEL, TN_PROJ), lambda i, j: (layer, 0, j)),
                  tile],
        out_specs=tile,
        out_shape=jax.ShapeDtypeStruct((n, D_MODEL), f32),
        compiler_params=_params(2), name="out_proj",
    )(a, w_all, x)


def _proj_res_split_kernel(a_ref, w_ref, xp_ref, xs_ref, o_ref):
    is_prompt = pl.program_id(0) < N_PROMPT // TM
    y = _dot(a_ref[...], w_ref[...].astype(bf16))

    @pl.when(is_prompt)
    def _():
        o_ref[...] = xp_ref[...] + y

    @pl.when(jnp.logical_not(is_prompt))
    def _():
        o_ref[...] = xs_ref[...] + y


def out_proj_residual_split(a, w_all, layer, x_p, x_s):
    p_spec, s_spec = _split_specs(TM, TN_PROJ)
    return pl.pallas_call(
        _proj_res_split_kernel,
        grid=(N_TOK // TM, D_MODEL // TN_PROJ),
        in_specs=[pl.BlockSpec((TM, D_MODEL), lambda i, j: (i, 0)),
                  pl.BlockSpec((None, D_MODEL, TN_PROJ), lambda i, j: (layer, 0, j)),
                  p_spec, s_spec],
        out_specs=pl.BlockSpec((TM, TN_PROJ), lambda i, j: (i, j)),
        out_shape=jax.ShapeDtypeStruct((N_TOK, D_MODEL), f32),
        compiler_params=_params(2), name="out_proj_split",
    )(a, w_all, x_p, x_s)


def _mixer_kernel(z_ref, st_ref, cw_ref, cb_ref, lg_ref, lb_ref, w_ref, sb_ref, ga_ref, gb_ref,
                  y_ref, vn_ref, tail_ref, prev_ref):
    i = pl.program_id(0)

    @pl.when(i == 0)
    def _():
        prev_ref[...] = jnp.zeros_like(prev_ref)

    is_sample = i >= N_PROMPT_BLOCKS
    sample_row = jnp.full((1, GROUP), i, jnp.int32) >= N_PROMPT_BLOCKS
    seq_start_row = (jnp.full((1, GROUP), i, jnp.int32) % BLOCKS_PER_SEQ) == 0
    mid = jnp.where(is_sample, DEC_SEQ, -8)
    row = lax.broadcasted_iota(jnp.int32, (ROWS, GROUP), 0)
    col = lax.broadcasted_iota(jnp.int32, (ROWS, GROUP), 1)
    sample_full = jnp.full((ROWS, GROUP), i, jnp.int32) >= N_PROMPT_BLOCKS
    same_seq = jnp.logical_or(jnp.logical_not(sample_full), (row // DEC_SEQ) == (col // DEC_SEQ))
    causal = jnp.logical_and(col <= row, same_seq)

    def group_rms(y, g):
        ms = jnp.mean(y * y, axis=-1, keepdims=True)
        return y * lax.rsqrt(ms + EPS) * g

    for grp in range(N_GROUPS):
        sl = slice(grp * GROUP, (grp + 1) * GROUP)
        gate_b = z_ref[:, grp * GROUP:(grp + 1) * GROUP]
        cin = (z_ref[:, D_CONV + grp * GROUP:D_CONV + (grp + 1) * GROUP]
               * z_ref[:, 2 * D_CONV + grp * GROUP:2 * D_CONV + (grp + 1) * GROUP])
        zero = jnp.zeros((1, GROUP), f32)
        a1 = jnp.where(sample_row, st_ref[0, 1:2, sl], jnp.where(seq_start_row, zero, prev_ref[7:8, sl]))
        a2 = jnp.where(sample_row, st_ref[0, 0:1, sl], jnp.where(seq_start_row, zero, prev_ref[6:7, sl]))
        b1 = st_ref[0, 3:4, sl]
        b2 = st_ref[0, 2:3, sl]
        s1 = pltpu.roll(cin, 1, 0)
        s2 = pltpu.roll(cin, 2, 0)
        s1 = jnp.where(row == 0, a1, s1)
        s1 = jnp.where(row == mid, b1, s1)
        s2 = jnp.where(row == 0, a2, jnp.where(row == 1, a1, s2))
        s2 = jnp.where(row == mid, b2, jnp.where(row == mid + 1, b1, s2))
        conv = cb_ref[:, sl] + cw_ref[0:1, sl] * s2 + cw_ref[1:2, sl] * s1 + cw_ref[2:3, sl] * cin
        y_ref[:, sl] = group_rms(gate_b * conv, ga_ref[:, sl]).astype(y_ref.dtype)
        tail_ref[0:8, sl] = cin[DEC_SEQ - 8:DEC_SEQ]
        tail_ref[8:16, sl] = cin[ROWS - 8:ROWS]
        prev_ref[:, sl] = cin[ROWS - 8:ROWS]

    v0 = 3 * D_CONV + D_SMLP
    u0 = 3 * D_CONV
    acc = z_ref[:, v0:v0 + GROUP]
    for grp in range(1, N_GROUPS):
        acc = acc + z_ref[:, v0 + grp * GROUP:v0 + (grp + 1) * GROUP]
    mean = jnp.sum(acc, axis=-1, keepdims=True) * (1.0 / D_SMLP)
    acc = jnp.zeros((ROWS, GROUP), f32)
    for grp in range(N_GROUPS):
        vc = z_ref[:, v0 + grp * GROUP:v0 + (grp + 1) * GROUP] - mean
        acc = acc + vc * vc
    rstd = lax.rsqrt(jnp.sum(acc, axis=-1, keepdims=True) * (1.0 / D_SMLP) + EPS)
    for grp in range(N_GROUPS):
        sl = slice(grp * GROUP, (grp + 1) * GROUP)
        vc = z_ref[:, v0 + grp * GROUP:v0 + (grp + 1) * GROUP] - mean
        vn = vc * rstd * lg_ref[:, sl] + lb_ref[:, sl]

        vn_ref[:, sl] = vn
        w = jnp.where(causal, w_ref[grp], 0.0).astype(bf16)
        s = _dot(w, vn.astype(bf16)) + sb_ref[:, sl]
        y_b = z_ref[:, u0 + grp * GROUP:u0 + (grp + 1) * GROUP] * s
        y_ref[:, D_CONV + grp * GROUP:D_CONV + (grp + 1) * GROUP] = (
            group_rms(y_b, gb_ref[:, sl]).astype(y_ref.dtype))


def mixers(z, state, conv_w, conv_b, ln_g, ln_b, smlp_w, smlp_b, g_a, g_b):
    half = DEC_SEQ
    w_modes = jnp.stack([smlp_w, jnp.tile(smlp_w[:, :half, :half], (1, 2, 2))])
    b_rows = jnp.stack([smlp_b.T, jnp.tile(smlp_b[:, :half], (1, 2)).T])
    b_modes = jnp.repeat(b_rows, GROUP, axis=2)
    st = state.reshape(N_SAMPLE_BLOCKS, 4, D_CONV)
    vec = lambda a: a.reshape(1, -1)
    vspec = pl.BlockSpec((1, D_CONV), lambda i: (0, 0))
    mode = lambda i: i // N_PROMPT_BLOCKS
    sblk = lambda i: jnp.maximum(i - N_PROMPT_BLOCKS, 0)
    return pl.pallas_call(
        _mixer_kernel,
        grid=(N_BLOCKS,),
        in_specs=[pl.BlockSpec((ROWS, D_IN_PROJ), lambda i: (i, 0)),
                  pl.BlockSpec((1, 4, D_CONV), lambda i: (sblk(i), 0, 0)),
                  pl.BlockSpec((3, D_CONV), lambda i: (0, 0)),
                  vspec, vspec, vspec,
                  pl.BlockSpec((None, N_GROUPS, ROWS, ROWS), lambda i: (mode(i), 0, 0, 0)),
                  pl.BlockSpec((None, ROWS, D_SMLP), lambda i: (mode(i), 0, 0)),
                  vspec, vspec],
        out_specs=[pl.BlockSpec((ROWS, 2 * D_CONV), lambda i: (i, 0)),
                   pl.BlockSpec((ROWS, D_SMLP), lambda i: (sblk(i), 0)),
                   pl.BlockSpec((16, D_CONV), lambda i: (i, 0))],
        out_shape=[jax.ShapeDtypeStruct((N_TOK, 2 * D_CONV), bf16),
                   jax.ShapeDtypeStruct((N_SAMPLE, D_SMLP), f32),
                   jax.ShapeDtypeStruct((N_BLOCKS * 16, D_CONV), f32)],
        scratch_shapes=[pltpu.VMEM((8, D_CONV), f32)],
        compiler_params=_params(1), name="mixers",
    )(z, st, conv_w, vec(conv_b), vec(ln_g), vec(ln_b), w_modes, b_modes, vec(g_a), vec(g_b))


def _swiglu_rows(a_ref, wg_ref, wu_ref, h_ref, rows):
    a = a_ref[0:rows, :]
    for c in range(TF // TF_SLAB):
        cs = slice(c * TF_SLAB, (c + 1) * TF_SLAB)
        g = _dot(a, wg_ref[:, cs].astype(bf16))
        u = _dot(a, wu_ref[:, cs].astype(bf16))
        h_ref[0:rows, cs] = (g * jax.nn.sigmoid(g) * u).astype(h_ref.dtype)


def _gate_up_kernel(a_ref, wg_ref, wu_ref, h_ref):
    _swiglu_rows(a_ref, wg_ref, wu_ref, h_ref, TM)


def _resident_rows_spec(index_map):
    return pl.BlockSpec((TM, D_MODEL), index_map, pipeline_mode=pl.Buffered(1))


def dense_gate_up(a, wg_all, wu_all, j):
    n = a.shape[0]
    wspec = pl.BlockSpec((None, D_MODEL, TF), lambda i, f: (j, 0, f))
    return pl.pallas_call(
        _gate_up_kernel,
        grid=(n // TM, D_FF // TF),
        in_specs=[_resident_rows_spec(lambda i, f: (i, 0)), wspec, wspec],
        out_specs=pl.BlockSpec((TM, TF), lambda i, f: (i, f)),
        out_shape=jax.ShapeDtypeStruct((n, D_FF), bf16),
        compiler_params=_params(2), name="dense_gate_up",
    )(a, wg_all, wu_all)


def _down_accumulate(h_ref, w_ref, o_ref, first):
    h = h_ref[...]
    for c in range(D_MODEL // TN_PROJ):
        cs = slice(c * TN_PROJ, (c + 1) * TN_PROJ)
        part = _dot(h, w_ref[:, cs].astype(bf16))
        if first:
            o_ref[:, cs] = part
        else:
            o_ref[:, cs] += part


def _down_kernel(h_ref, w_ref, o_ref):
    @pl.when(pl.program_id(1) == 0)
    def _():
        _down_accumulate(h_ref, w_ref, o_ref, True)

    @pl.when(pl.program_id(1) > 0)
    def _():
        _down_accumulate(h_ref, w_ref, o_ref, False)


def dense_down(h, wd_all, j):
    n = h.shape[0]
    return pl.pallas_call(
        _down_kernel,
        grid=(n // TM, D_FF // TK),
        in_specs=[pl.BlockSpec((TM, TK), lambda i, k: (i, k)),
                  pl.BlockSpec((None, TK, D_MODEL), lambda i, k: (j, k, 0))],
        out_specs=pl.BlockSpec((TM, D_MODEL), lambda i, k: (i, 0)),
        out_shape=jax.ShapeDtypeStruct((n, D_MODEL), f32),
        compiler_params=_params(2), name="dense_down",
    )(h, wd_all)


def _router_kernel(x_ref, g_ref, w_ref, b_ref, o_ref):
    xn = _rms(x_ref[...], g_ref[...]).astype(bf16)
    logits = _dot(xn, w_ref[...].astype(bf16)) + b_ref[...]
    lane = lax.broadcasted_iota(jnp.int32, logits.shape, 1)
    neg = jnp.float32(-jnp.inf)
    logits = jnp.where(lane < N_EXPERTS, logits, neg)
    m1 = jnp.max(logits, axis=-1, keepdims=True)
    i1 = jnp.min(jnp.where(logits == m1, lane, GROUP), axis=-1, keepdims=True)
    rest = jnp.where(lane == i1, neg, logits)
    m2 = jnp.max(rest, axis=-1, keepdims=True)
    i2 = jnp.min(jnp.where(rest == m2, lane, GROUP), axis=-1, keepdims=True)
    e = jnp.exp(m2 - m1)
    w1 = 1.0 / (1.0 + e)
    w2 = e / (1.0 + e)
    out = jnp.where(lane == 0, i1.astype(f32),
                    jnp.where(lane == 1, i2.astype(f32),
                              jnp.where(lane == 2, w1, jnp.where(lane == 3, w2, 0.0))))
    o_ref[...] = out


def router(x, g, w_router, b_router):
    n = x.shape[0]
    w_pad = jnp.pad(w_router, ((0, 0), (0, GROUP - N_EXPERTS)))
    b_pad = jnp.pad(b_router, (0, GROUP - N_EXPERTS)).reshape(1, GROUP)
    return pl.pallas_call(
        _router_kernel,
        grid=(n // TR,),
        in_specs=[pl.BlockSpec((TR, D_MODEL), lambda i: (i, 0)),
                  pl.BlockSpec((1, D_MODEL), lambda i: (0, 0)),
                  pl.BlockSpec((D_MODEL, GROUP), lambda i: (0, 0)),
                  pl.BlockSpec((1, GROUP), lambda i: (0, 0))],
        out_specs=pl.BlockSpec((TR, GROUP), lambda i: (i, 0)),
        out_shape=jax.ShapeDtypeStruct((n, GROUP), f32),
        compiler_params=_params(1), name="router",
    )(x, g.reshape(1, D_MODEL), w_pad, b_pad)


def _start_row_gather(idx_ref, base, src_hbm, dst_ref, sem):
    def issue(r8, carry):
        for k in range(8):
            r = r8 * 8 + k
            pltpu.make_async_copy(src_hbm.at[pl.ds(idx_ref[base + r], 1)],
                                  dst_ref.at[pl.ds(r, 1)], sem).start(priority=k % 2)
        return carry
    lax.fori_loop(0, TG // 8, issue, 0)


def _wait_row_gather(src_hbm, dst_ref, sem):
    pltpu.make_async_copy(src_hbm.at[pl.ds(0, TG)], dst_ref, sem).wait()


def _dispatch_kernel(tok_ref, np_ref, x_hbm, g_ref, o_ref, buf_ref, sem):
    i = pl.program_id(0)
    n = pl.num_programs(0)

    def has_rows(step):
        return (step % PIECES) < np_ref[step // PIECES]

    def start(step):
        slot = step % 2
        _start_row_gather(tok_ref, step * TG, x_hbm, buf_ref.at[slot], sem.at[slot])

    @pl.when(jnp.logical_and(i == 0, has_rows(0)))
    def _():
        start(0)

    nxt = jnp.minimum(i + 1, n - 1)

    @pl.when(jnp.logical_and(i + 1 < n, has_rows(nxt)))
    def _():
        start(nxt)

    @pl.when(has_rows(i))
    def _():
        slot = i % 2
        _wait_row_gather(x_hbm, buf_ref.at[slot], sem.at[slot])
        o_ref[...] = _rms(buf_ref[slot], g_ref[...]).astype(o_ref.dtype)

    @pl.when(jnp.logical_not(has_rows(i)))
    def _():
        o_ref[...] = jnp.zeros_like(o_ref)


def dispatch(x, g, token_of_row, n_pieces):
    return pl.pallas_call(
        _dispatch_kernel,
        grid_spec=pltpu.PrefetchScalarGridSpec(
            num_scalar_prefetch=2,
            grid=(P_ROWS // TG,),
            in_specs=[pl.BlockSpec(memory_space=pl.ANY),
                      pl.BlockSpec((1, D_MODEL), lambda i, tok, npc: (0, 0))],
            out_specs=pl.BlockSpec((TG, D_MODEL), lambda i, tok, npc: (i, 0)),
            scratch_shapes=[pltpu.VMEM((2, TG, D_MODEL), f32), pltpu.SemaphoreType.DMA((2,))]),
        out_shape=jax.ShapeDtypeStruct((P_ROWS, D_MODEL), bf16),
        compiler_params=_params(1), name="moe_dispatch",
    )(token_of_row, n_pieces, x, g.reshape(1, D_MODEL))


def _moe_gate_up_kernel(be_ref, np_ref, a_ref, wg_ref, wu_ref, h_ref):
    n_pieces = np_ref[pl.program_id(0)]
    for q in range(PIECES + 1):
        @pl.when(n_pieces == q)
        def _():
            rows = q * TG
            if rows > 0:
                _swiglu_rows(a_ref, wg_ref, wu_ref, h_ref, rows)
            if rows < TM:
                h_ref[rows:TM, :] = jnp.zeros((TM - rows, TF), h_ref.dtype)


def _input_step(np_ref, b, s, n_inner):
    last = np_ref[N_MOE_BLOCKS] - 1
    return jnp.minimum(b, last), jnp.where(b <= last, s, n_inner - 1)


def moe_gate_up(a, wg_all, wu_all, j, block_expert, n_pieces):
    n_f = D_FF // TF

    def w_index(b, f, be, npc):
        _, ff = _input_step(npc, b, f, n_f)
        return j, be[b], 0, ff

    wspec = pl.BlockSpec((None, None, D_MODEL, TF), w_index)
    return pl.pallas_call(
        _moe_gate_up_kernel,
        grid_spec=pltpu.PrefetchScalarGridSpec(
            num_scalar_prefetch=2,
            grid=(N_MOE_BLOCKS, n_f),
            in_specs=[_resident_rows_spec(lambda b, f, be, npc: (_input_step(npc, b, f, n_f)[0], 0)),
                      wspec, wspec],
            out_specs=pl.BlockSpec((TM, TF), lambda b, f, be, npc: (b, f))),
        out_shape=jax.ShapeDtypeStruct((P_ROWS, D_FF), bf16),
        compiler_params=_params(2), name="moe_gate_up",
    )(block_expert, n_pieces, a, wg_all, wu_all)


def _down_accumulate_rows(h_ref, w_ref, o_ref, rows, first):
    if rows > 0:
        h = h_ref[0:rows, :]
        for c in range(D_MODEL // TN_PROJ):
            cs = slice(c * TN_PROJ, (c + 1) * TN_PROJ)
            part = _dot(h, w_ref[:, cs].astype(bf16))
            if first:
                o_ref[0:rows, cs] = part
            else:
                o_ref[0:rows, cs] += part
    if first and rows < TM:
        o_ref[rows:TM, :] = jnp.zeros((TM - rows, D_MODEL), o_ref.dtype)


def _moe_down_kernel(be_ref, np_ref, h_ref, w_ref, o_ref):
    n_pieces = np_ref[pl.program_id(0)]
    k = pl.program_id(1)
    for q in range(PIECES + 1):
        @pl.when(jnp.logical_and(n_pieces == q, k == 0))
        def _():
            _down_accumulate_rows(h_ref, w_ref, o_ref, q * TG, True)

        if q > 0:
            @pl.when(jnp.logical_and(n_pieces == q, k > 0))
            def _():
                _down_accumulate_rows(h_ref, w_ref, o_ref, q * TG, False)


def moe_down(h, wd_all, j, block_expert, n_pieces):
    n_k = D_FF // TK

    def w_index(b, k, be, npc):
        _, kk = _input_step(npc, b, k, n_k)
        return j, be[b], kk, 0

    return pl.pallas_call(
        _moe_down_kernel,
        grid_spec=pltpu.PrefetchScalarGridSpec(
            num_scalar_prefetch=2,
            grid=(N_MOE_BLOCKS, n_k),
            in_specs=[pl.BlockSpec((TM, TK), lambda b, k, be, npc: _input_step(npc, b, k, n_k)),
                      pl.BlockSpec((None, None, TK, D_MODEL), w_index)],
            out_specs=pl.BlockSpec((TM, D_MODEL), lambda b, k, be, npc: (b, 0))),
        out_shape=jax.ShapeDtypeStruct((P_ROWS, D_MODEL), f32),
        compiler_params=_params(2), name="moe_down",
    )(block_expert, n_pieces, h, wd_all)


def _combine_kernel(p1_ref, p2_ref, x_ref, r_ref, y_hbm, g_ref, op_ref, os_ref, buf_ref, sem):
    i = pl.program_id(0)
    n = pl.num_programs(0)

    def start(step):
        slot = step % 2
        _start_row_gather(p1_ref, step * TG, y_hbm, buf_ref.at[slot, 0], sem.at[slot])
        _start_row_gather(p2_ref, step * TG, y_hbm, buf_ref.at[slot, 1], sem.at[slot])

    @pl.when(i == 0)
    def _():
        start(0)

    @pl.when(i + 1 < n)
    def _():
        start(i + 1)

    slot = i % 2
    _wait_row_gather(y_hbm, buf_ref.at[slot, 0], sem.at[slot])
    _wait_row_gather(y_hbm, buf_ref.at[slot, 1], sem.at[slot])

    def result():
        x = x_ref[...] + (r_ref[:, 2:3] * buf_ref[slot, 0] + r_ref[:, 3:4] * buf_ref[slot, 1])
        return _rms(x, g_ref[...])

    is_prompt = i < N_PROMPT // TG

    @pl.when(is_prompt)
    def _():
        op_ref[...] = result()

    @pl.when(jnp.logical_not(is_prompt))
    def _():
        os_ref[...] = result()


def combine_final_norm(x, routing, y_rows, pos1, pos2, g):
    n_p = N_PROMPT // TG
    return pl.pallas_call(
        _combine_kernel,
        grid_spec=pltpu.PrefetchScalarGridSpec(
            num_scalar_prefetch=2,
            grid=(N_TOK // TG,),
            in_specs=[pl.BlockSpec((TG, D_MODEL), lambda i, p1, p2: (i, 0)),
                      pl.BlockSpec((TG, GROUP), lambda i, p1, p2: (i, 0)),
                      pl.BlockSpec(memory_space=pl.ANY),
                      pl.BlockSpec((1, D_MODEL), lambda i, p1, p2: (0, 0))],
            out_specs=[pl.BlockSpec((TG, D_MODEL), lambda i, p1, p2: (jnp.minimum(i, n_p - 1), 0)),
                       pl.BlockSpec((TG, D_MODEL), lambda i, p1, p2: (jnp.maximum(i - n_p, 0), 0))],
            scratch_shapes=[pltpu.VMEM((2, 2, TG, D_MODEL), f32), pltpu.SemaphoreType.DMA((2,))]),
        out_shape=[jax.ShapeDtypeStruct((N_PROMPT, D_MODEL), f32),
                   jax.ShapeDtypeStruct((N_SAMPLE, D_MODEL), f32)],
        compiler_params=_params(1), name="moe_combine",
    )(pos1, pos2, x, routing, y_rows, g.reshape(1, D_MODEL))


def _routing_tables(routing):
    experts = routing[:, :TOP_K].astype(jnp.int32).reshape(-1)
    onehot = (experts[:, None] == jnp.arange(N_EXPERTS, dtype=jnp.int32)[None, :]).astype(jnp.int32)
    csum = jnp.cumsum(onehot, axis=0)
    rank = jnp.sum(onehot * (csum - 1), axis=1)
    counts = csum[-1]
    blocks = (counts + TM - 1) // TM
    block_end = jnp.cumsum(blocks)
    block_start = block_end - blocks
    pos = (block_start[experts] * TM + rank).astype(jnp.int32)
    n_used = block_end[-1]
    blk = jnp.arange(N_MOE_BLOCKS, dtype=jnp.int32)
    block_expert = jnp.minimum(
        jnp.sum((jnp.minimum(blk, n_used - 1)[:, None] >= block_end[None, :]).astype(jnp.int32), axis=1),
        N_EXPERTS - 1).astype(jnp.int32)
    rows_in_block = jnp.clip(counts[block_expert] - (blk - block_start[block_expert]) * TM, 0, TM)
    rows_in_block = jnp.where(blk < n_used, rows_in_block, 0)
    n_pieces = jnp.concatenate([(rows_in_block + TG - 1) // TG, n_used[None]]).astype(jnp.int32)
    token = jnp.arange(TOP_K * N_TOK, dtype=jnp.int32) // TOP_K
    token_of_row = jnp.zeros((P_ROWS,), jnp.int32).at[pos].set(
        token, unique_indices=True, mode="promise_in_bounds")
    pos = pos.reshape(N_TOK, TOP_K)
    return token_of_row, pos[:, 0], pos[:, 1], block_expert, n_pieces


def kernel(x_prompt, x_sample, state_conv, ln1_g, w_in, conv_w, conv_b, v_ln_g, v_ln_b, smlp_w, smlp_b, out_norm_a_g, out_norm_b_g, w_out, ln2_g, ffn_w_gate, ffn_w_up, ffn_w_down, router_w, router_b, moe_w_gate, moe_w_up, moe_w_down, final_norm_g):
    assert DEPTH == 2
    x_p = x_prompt.reshape(N_PROMPT, D_MODEL)
    x_s = x_sample.reshape(N_SAMPLE, D_MODEL)
    xn = rms_norm_split(x_p, x_s, ln1_g[0])
    conv_p, conv_s, v_rows = [], [], []
    x = y_p = y_s = None
    for l in range(DEPTH):
        z = in_proj(xn, w_in, l)
        ymix, vn, tails = mixers(z, state_conv[l], conv_w[l], conv_b[l], v_ln_g[l], v_ln_b[l],
                                 smlp_w[l], smlp_b[l], out_norm_a_g[l], out_norm_b_g[l])
        if l == 0:
            x = out_proj_residual_split(ymix, w_out, l, x_p, x_s)
        else:
            x = out_proj_residual(ymix, w_out, l, x)
        conv_p.append(jnp.stack([tails[(BLOCKS_PER_SEQ * (b + 1) - 1) * 16 + 14:BLOCKS_PER_SEQ * (b + 1) * 16]
                                 for b in range(BATCH)]))
        conv_s.append(tails[N_PROMPT_BLOCKS * 16:].reshape(DEC_BATCH, 8, D_CONV)[:, 6:8])
        v_rows.append(vn.reshape(DEC_BATCH, DEC_SEQ, D_SMLP))
        j = l // 2
        if l % 2 == 0:
            xn2 = rms_norm(x, ln2_g[l], bf16)
            h = dense_gate_up(xn2, ffn_w_gate, ffn_w_up, j)
            d = dense_down(h, ffn_w_down, j)
            x, xn = add_rms_norm(x, d, ln1_g[l + 1])
        else:
            routing = router(x, ln2_g[l], router_w[j], router_b[j])
            token_of_row, pos1, pos2, block_expert, n_pieces = _routing_tables(routing)
            rows = dispatch(x, ln2_g[l], token_of_row, n_pieces)
            h = moe_gate_up(rows, moe_w_gate, moe_w_up, j, block_expert, n_pieces)
            y_rows = moe_down(h, moe_w_down, j, block_expert, n_pieces)
            y_p, y_s = combine_final_norm(x, routing, y_rows, pos1, pos2, final_norm_g)
    return (y_p.reshape(BATCH, SEQ, D_MODEL), y_s.reshape(DEC_BATCH, DEC_SEQ, D_MODEL),
            jnp.stack(conv_p), jnp.stack(conv_s), jnp.stack(v_rows))
```

```python
import jax
import jax.numpy as jnp
from jax import lax
from jax.experimental import pallas as pl
from jax.experimental.pallas import tpu as pltpu

D_MODEL = 4096
BATCH = 2
SEQ = 4096
DEPTH = 2
DEC_BATCH = 32
DEC_SEQ = 64
D_CONV = 2048
D_SMLP = 2048
N_GROUPS = 16
GROUP = 128
SMLP_CHUNK = 128
D_IN_PROJ = 3 * D_CONV + 2 * D_SMLP
D_FF = 14336
N_EXPERTS = 8
TOP_K = 2
EPS = 1e-5

N_PROMPT = BATCH * SEQ
N_SAMPLE = DEC_BATCH * DEC_SEQ
N_TOK = N_PROMPT + N_SAMPLE
ROWS = SMLP_CHUNK
N_PROMPT_BLOCKS = N_PROMPT // ROWS
N_SAMPLE_BLOCKS = N_SAMPLE // ROWS
N_BLOCKS = N_PROMPT_BLOCKS + N_SAMPLE_BLOCKS
BLOCKS_PER_SEQ = SEQ // ROWS

V7X_VMEM_LIMIT_BYTES = 60 * 1024 * 1024

TM = 1024
TN_PROJ = 512
TN_IN = 1024
TF = 512
TF_SLAB = 256
TK = 512
TR = 256
TG = 256
PIECE = 128
P_ROWS = TOP_K * N_TOK + N_EXPERTS * TM
N_MOE_BLOCKS = P_ROWS // TM
PIECES = TM // PIECE
STEPS_PER_BLOCK = TM // TG

bf16 = jnp.bfloat16
f32 = jnp.float32


def _params(n_axes):
    return pltpu.CompilerParams(dimension_semantics=("arbitrary",) * n_axes,
                                vmem_limit_bytes=V7X_VMEM_LIMIT_BYTES)


def _rms(x, g):
    ms = jnp.mean(x * x, axis=-1, keepdims=True)
    return x * lax.rsqrt(ms + EPS) * g


def _dot(a, w):
    return jnp.dot(a, w, preferred_element_type=f32)


def _rms_kernel(x_ref, g_ref, o_ref):
    o_ref[...] = _rms(x_ref[...], g_ref[...]).astype(o_ref.dtype)


def rms_norm(x, g, out_dtype):
    n = x.shape[0]
    return pl.pallas_call(
        _rms_kernel,
        grid=(n // TR,),
        in_specs=[pl.BlockSpec((TR, D_MODEL), lambda i: (i, 0)),
                  pl.BlockSpec((1, D_MODEL), lambda i: (0, 0))],
        out_specs=pl.BlockSpec((TR, D_MODEL), lambda i: (i, 0)),
        out_shape=jax.ShapeDtypeStruct((n, D_MODEL), out_dtype),
        compiler_params=_params(1), name="rms_norm",
    )(x, g.reshape(1, D_MODEL))


def _split_specs(rows, width):
    n_p = N_PROMPT // rows
    col = (lambda i, j: j) if width != D_MODEL else (lambda i, j: 0)
    prompt = lambda i, j=0: (jnp.minimum(i, n_p - 1), jnp.where(i < n_p, col(i, j), 0))
    sample = lambda i, j=0: (jnp.maximum(i - n_p, 0), jnp.where(i < n_p, 0, col(i, j)))
    return pl.BlockSpec((rows, width), prompt), pl.BlockSpec((rows, width), sample)


def _rms_split_kernel(xp_ref, xs_ref, g_ref, o_ref):
    is_prompt = pl.program_id(0) < N_PROMPT // TR

    @pl.when(is_prompt)
    def _():
        o_ref[...] = _rms(xp_ref[...], g_ref[...]).astype(o_ref.dtype)

    @pl.when(jnp.logical_not(is_prompt))
    def _():
        o_ref[...] = _rms(xs_ref[...], g_ref[...]).astype(o_ref.dtype)


def rms_norm_split(x_p, x_s, g):
    p_spec, s_spec = _split_specs(TR, D_MODEL)
    return pl.pallas_call(
        _rms_split_kernel,
        grid=(N_TOK // TR,),
        in_specs=[p_spec, s_spec, pl.BlockSpec((1, D_MODEL), lambda i: (0, 0))],
        out_specs=pl.BlockSpec((TR, D_MODEL), lambda i: (i, 0)),
        out_shape=jax.ShapeDtypeStruct((N_TOK, D_MODEL), bf16),
        compiler_params=_params(1), name="rms_norm_split",
    )(x_p, x_s, g.reshape(1, D_MODEL))


def _add_rms_kernel(x_ref, d_ref, g_ref, xo_ref, no_ref):
    x = x_ref[...] + d_ref[...]
    xo_ref[...] = x
    no_ref[...] = _rms(x, g_ref[...]).astype(no_ref.dtype)


def add_rms_norm(x, d, g):
    n = x.shape[0]
    row = pl.BlockSpec((TR, D_MODEL), lambda i: (i, 0))
    return pl.pallas_call(
        _add_rms_kernel,
        grid=(n // TR,),
        in_specs=[row, row, pl.BlockSpec((1, D_MODEL), lambda i: (0, 0))],
        out_specs=[row, row],
        out_shape=[jax.ShapeDtypeStruct((n, D_MODEL), f32),
                   jax.ShapeDtypeStruct((n, D_MODEL), bf16)],
        compiler_params=_params(1), name="add_rms_norm",
    )(x, d, g.reshape(1, D_MODEL))


def _proj_kernel(a_ref, w_ref, o_ref):
    a = a_ref[...]
    for c in range(TN_IN // TN_PROJ):
        cs = slice(c * TN_PROJ, (c + 1) * TN_PROJ)
        o_ref[:, cs] = _dot(a, w_ref[:, cs].astype(bf16))


def in_proj(a, w_all, layer):
    n = a.shape[0]
    return pl.pallas_call(
        _proj_kernel,
        grid=(n // TM, D_IN_PROJ // TN_IN),
        in_specs=[_resident_rows_spec(lambda i, j: (i, 0)),
                  pl.BlockSpec((None, D_MODEL, TN_IN), lambda i, j: (layer, 0, j))],
        out_specs=pl.BlockSpec((TM, TN_IN), lambda i, j: (i, j)),
        out_shape=jax.ShapeDtypeStruct((n, D_IN_PROJ), f32),
        compiler_params=_params(2), name="in_proj",
    )(a, w_all)


def _proj_res_kernel(a_ref, w_ref, x_ref, o_ref):
    o_ref[...] = x_ref[...] + _dot(a_ref[...], w_ref[...].astype(bf16))


def out_proj_residual(a, w_all, layer, x):
    n = a.shape[0]
    tile = pl.BlockSpec((TM, TN_PROJ), lambda i, j: (i, j))
    return pl.pallas_call(
        _proj_res_kernel,
        grid=(n // TM, D_MODEL // TN_PROJ),
        in_specs=[pl.BlockSpec((TM, D_MODEL), lambda i, j: (i, 0)),
                  pl.BlockSpec((None, D_MODEL, TN_PROJ), lambda i, j: (layer, 0, j)),
                  tile],
        out_specs=tile,
        out_shape=jax.ShapeDtypeStruct((n, D_MODEL), f32),
        compiler_params=_params(2), name="out_proj",
    )(a, w_all, x)


def _proj_res_split_kernel(a_ref, w_ref, xp_ref, xs_ref, o_ref):
    is_prompt = pl.program_id(0) < N_PROMPT // TM
    y = _dot(a_ref[...], w_ref[...].astype(bf16))

    @pl.when(is_prompt)
    def _():
        o_ref[...] = xp_ref[...] + y

    @pl.when(jnp.logical_not(is_prompt))
    def _():
        o_ref[...] = xs_ref[...] + y


def out_proj_residual_split(a, w_all, layer, x_p, x_s):
    p_spec, s_spec = _split_specs(TM, TN_PROJ)
    return pl.pallas_call(
        _proj_res_split_kernel,
        grid=(N_TOK // TM, D_MODEL // TN_PROJ),
        in_specs=[pl.BlockSpec((TM, D_MODEL), lambda i, j: (i, 0)),
                  pl.BlockSpec((None, D_MODEL, TN_PROJ), lambda i, j: (layer, 0, j)),
                  p_spec, s_spec],
        out_specs=pl.BlockSpec((TM, TN_PROJ), lambda i, j: (i, j)),
        out_shape=jax.ShapeDtypeStruct((N_TOK, D_MODEL), f32),
        compiler_params=_params(2), name="out_proj_split",
    )(a, w_all, x_p, x_s)


def _mixer_kernel(z_ref, st_ref, cw_ref, cb_ref, lg_ref, lb_ref, w_ref, sb_ref, ga_ref, gb_ref,
                  y_ref, vn_ref, tail_ref, prev_ref):
    i = pl.program_id(0)

    @pl.when(i == 0)
    def _():
        prev_ref[...] = jnp.zeros_like(prev_ref)

    is_sample = i >= N_PROMPT_BLOCKS
    sample_row = jnp.full((1, GROUP), i, jnp.int32) >= N_PROMPT_BLOCKS
    seq_start_row = (jnp.full((1, GROUP), i, jnp.int32) % BLOCKS_PER_SEQ) == 0
    mid = jnp.where(is_sample, DEC_SEQ, -8)
    row = lax.broadcasted_iota(jnp.int32, (ROWS, GROUP), 0)
    col = lax.broadcasted_iota(jnp.int32, (ROWS, GROUP), 1)
    sample_full = jnp.full((ROWS, GROUP), i, jnp.int32) >= N_PROMPT_BLOCKS
    same_seq = jnp.logical_or(jnp.logical_not(sample_full), (row // DEC_SEQ) == (col // DEC_SEQ))
    causal = jnp.logical_and(col <= row, same_seq)

    def group_rms(y, g):
        ms = jnp.mean(y * y, axis=-1, keepdims=True)
        return y * lax.rsqrt(ms + EPS) * g

    for grp in range(N_GROUPS):
        sl = slice(grp * GROUP, (grp + 1) * GROUP)
        gate_b = z_ref[:, grp * GROUP:(grp + 1) * GROUP]
        cin = (z_ref[:, D_CONV + grp * GROUP:D_CONV + (grp + 1) * GROUP]
               * z_ref[:, 2 * D_CONV + grp * GROUP:2 * D_CONV + (grp + 1) * GROUP])
        zero = jnp.zeros((1, GROUP), f32)
        a1 = jnp.where(sample_row, st_ref[0, 1:2, sl], jnp.where(seq_start_row, zero, prev_ref[7:8, sl]))
        a2 = jnp.where(sample_row, st_ref[0, 0:1, sl], jnp.where(seq_start_row, zero, prev_ref[6:7, sl]))
        b1 = st_ref[0, 3:4, sl]
        b2 = st_ref[0, 2:3, sl]
        s1 = pltpu.roll(cin, 1, 0)
        s2 = pltpu.roll(cin, 2, 0)
        s1 = jnp.where(row == 0, a1, s1)
        s1 = jnp.where(row == mid, b1, s1)
        s2 = jnp.where(row == 0, a2, jnp.where(row == 1, a1, s2))
        s2 = jnp.where(row == mid, b2, jnp.where(row == mid + 1, b1, s2))
        conv = cb_ref[:, sl] + cw_ref[0:1, sl] * s2 + cw_ref[1:2, sl] * s1 + cw_ref[2:3, sl] * cin
        y_ref[:, sl] = group_rms(gate_b * conv, ga_ref[:, sl]).astype(y_ref.dtype)
        tail_ref[0:8, sl] = cin[DEC_SEQ - 8:DEC_SEQ]
        tail_ref[8:16, sl] = cin[ROWS - 8:ROWS]
        prev_ref[:, sl] = cin[ROWS - 8:ROWS]

    v0 = 3 * D_CONV + D_SMLP
    u0 = 3 * D_CONV
    acc = z_ref[:, v0:v0 + GROUP]
    for grp in range(1, N_GROUPS):
        acc = acc + z_ref[:, v0 + grp * GROUP:v0 + (grp + 1) * GROUP]
    mean = jnp.sum(acc, axis=-1, keepdims=True) * (1.0 / D_SMLP)
    acc = jnp.zeros((ROWS, GROUP), f32)
    for grp in range(N_GROUPS):
        vc = z_ref[:, v0 + grp * GROUP:v0 + (grp + 1) * GROUP] - mean
        acc = acc + vc * vc
    rstd = lax.rsqrt(jnp.sum(acc, axis=-1, keepdims=True) * (1.0 / D_SMLP) + EPS)
    for grp in range(N_GROUPS):
        sl = slice(grp * GROUP, (grp + 1) * GROUP)
        vc = z_ref[:, v0 + grp * GROUP:v0 + (grp + 1) * GROUP] - mean
        vn = vc * rstd * lg_ref[:, sl] + lb_ref[:, sl]

        vn_ref[:, sl] = vn
        w = jnp.where(causal, w_ref[grp], 0.0).astype(bf16)
        s = _dot(w, vn.astype(bf16)) + sb_ref[:, sl]
        y_b = z_ref[:, u0 + grp * GROUP:u0 + (grp + 1) * GROUP] * s
        y_ref[:, D_CONV + grp * GROUP:D_CONV + (grp + 1) * GROUP] = (
            group_rms(y_b, gb_ref[:, sl]).astype(y_ref.dtype))


def mixers(z, state, conv_w, conv_b, ln_g, ln_b, smlp_w, smlp_b, g_a, g_b):
    half = DEC_SEQ
    w_modes = jnp.stack([smlp_w, jnp.tile(smlp_w[:, :half, :half], (1, 2, 2))])
    b_rows = jnp.stack([smlp_b.T, jnp.tile(smlp_b[:, :half], (1, 2)).T])
    b_modes = jnp.repeat(b_rows, GROUP, axis=2)
    st = state.reshape(N_SAMPLE_BLOCKS, 4, D_CONV)
    vec = lambda a: a.reshape(1, -1)
    vspec = pl.BlockSpec((1, D_CONV), lambda i: (0, 0))
    mode = lambda i: i // N_PROMPT_BLOCKS
    sblk = lambda i: jnp.maximum(i - N_PROMPT_BLOCKS, 0)
    return pl.pallas_call(
        _mixer_kernel,
        grid=(N_BLOCKS,),
        in_specs=[pl.BlockSpec((ROWS, D_IN_PROJ), lambda i: (i, 0)),
                  pl.BlockSpec((1, 4, D_CONV), lambda i: (sblk(i), 0, 0)),
                  pl.BlockSpec((3, D_CONV), lambda i: (0, 0)),
                  vspec, vspec, vspec,
                  pl.BlockSpec((None, N_GROUPS, ROWS, ROWS), lambda i: (mode(i), 0, 0, 0)),
                  pl.BlockSpec((None, ROWS, D_SMLP), lambda i: (mode(i), 0, 0)),
                  vspec, vspec],
        out_specs=[pl.BlockSpec((ROWS, 2 * D_CONV), lambda i: (i, 0)),
                   pl.BlockSpec((ROWS, D_SMLP), lambda i: (sblk(i), 0)),
                   pl.BlockSpec((16, D_CONV), lambda i: (i, 0))],
        out_shape=[jax.ShapeDtypeStruct((N_TOK, 2 * D_CONV), bf16),
                   jax.ShapeDtypeStruct((N_SAMPLE, D_SMLP), f32),
                   jax.ShapeDtypeStruct((N_BLOCKS * 16, D_CONV), f32)],
        scratch_shapes=[pltpu.VMEM((8, D_CONV), f32)],
        compiler_params=_params(1), name="mixers",
    )(z, st, conv_w, vec(conv_b), vec(ln_g), vec(ln_b), w_modes, b_modes, vec(g_a), vec(g_b))


def _swiglu_rows(a_ref, wg_ref, wu_ref, h_ref, rows):
    a = a_ref[0:rows, :]
    for c in range(TF // TF_SLAB):
        cs = slice(c * TF_SLAB, (c + 1) * TF_SLAB)
        g = _dot(a, wg_ref[:, cs].astype(bf16))
        u = _dot(a, wu_ref[:, cs].astype(bf16))
        h_ref[0:rows, cs] = (g * jax.nn.sigmoid(g) * u).astype(h_ref.dtype)


def _gate_up_kernel(a_ref, wg_ref, wu_ref, h_ref):
    _swiglu_rows(a_ref, wg_ref, wu_ref, h_ref, TM)


def _resident_rows_spec(index_map):
    return pl.BlockSpec((TM, D_MODEL), index_map, pipeline_mode=pl.Buffered(1))


def dense_gate_up(a, wg_all, wu_all, j):
    n = a.shape[0]
    wspec = pl.BlockSpec((None, D_MODEL, TF), lambda i, f: (j, 0, f))
    return pl.pallas_call(
        _gate_up_kernel,
        grid=(n // TM, D_FF // TF),
        in_specs=[_resident_rows_spec(lambda i, f: (i, 0)), wspec, wspec],
        out_specs=pl.BlockSpec((TM, TF), lambda i, f: (i, f)),
        out_shape=jax.ShapeDtypeStruct((n, D_FF), bf16),
        compiler_params=_params(2), name="dense_gate_up",
    )(a, wg_all, wu_all)


def _down_accumulate(h_ref, w_ref, o_ref, first):
    h = h_ref[...]
    for c in range(D_MODEL // TN_PROJ):
        cs = slice(c * TN_PROJ, (c + 1) * TN_PROJ)
        part = _dot(h, w_ref[:, cs].astype(bf16))
        if first:
            o_ref[:, cs] = part
        else:
            o_ref[:, cs] += part


def _down_kernel(h_ref, w_ref, o_ref):
    @pl.when(pl.program_id(1) == 0)
    def _():
        _down_accumulate(h_ref, w_ref, o_ref, True)

    @pl.when(pl.program_id(1) > 0)
    def _():
        _down_accumulate(h_ref, w_ref, o_ref, False)


def dense_down(h, wd_all, j):
    n = h.shape[0]
    return pl.pallas_call(
        _down_kernel,
        grid=(n // TM, D_FF // TK),
        in_specs=[pl.BlockSpec((TM, TK), lambda i, k: (i, k)),
                  pl.BlockSpec((None, TK, D_MODEL), lambda i, k: (j, k, 0))],
        out_specs=pl.BlockSpec((TM, D_MODEL), lambda i, k: (i, 0)),
        out_shape=jax.ShapeDtypeStruct((n, D_MODEL), f32),
        compiler_params=_params(2), name="dense_down",
    )(h, wd_all)


HALF = D_MODEL // 2
HIGH16 = 0xFFFF0000


def _pack_bf16_pairs(xn):
    lo = lax.bitcast_convert_type(xn[:, :HALF].astype(f32), jnp.uint32) >> 16
    hi = lax.bitcast_convert_type(xn[:, HALF:].astype(f32), jnp.uint32) & jnp.uint32(HIGH16)
    return hi | lo


def _unpack_bf16_pairs(words, o_ref):
    o_ref[:, :HALF] = lax.bitcast_convert_type(words << 16, f32).astype(o_ref.dtype)
    o_ref[:, HALF:] = lax.bitcast_convert_type(words & jnp.uint32(HIGH16), f32).astype(o_ref.dtype)


def _router_kernel(x_ref, g_ref, w_ref, b_ref, o_ref, p_ref):
    xn = _rms(x_ref[...], g_ref[...]).astype(bf16)
    p_ref[...] = _pack_bf16_pairs(xn)
    logits = _dot(xn, w_ref[...].astype(bf16)) + b_ref[...]
    lane = lax.broadcasted_iota(jnp.int32, logits.shape, 1)
    neg = jnp.float32(-jnp.inf)
    logits = jnp.where(lane < N_EXPERTS, logits, neg)
    m1 = jnp.max(logits, axis=-1, keepdims=True)
    i1 = jnp.min(jnp.where(logits == m1, lane, GROUP), axis=-1, keepdims=True)
    rest = jnp.where(lane == i1, neg, logits)
    m2 = jnp.max(rest, axis=-1, keepdims=True)
    i2 = jnp.min(jnp.where(rest == m2, lane, GROUP), axis=-1, keepdims=True)
    e = jnp.exp(m2 - m1)
    w1 = 1.0 / (1.0 + e)
    w2 = e / (1.0 + e)
    out = jnp.where(lane == 0, i1.astype(f32),
                    jnp.where(lane == 1, i2.astype(f32),
                              jnp.where(lane == 2, w1, jnp.where(lane == 3, w2, 0.0))))
    o_ref[...] = out


def router(x, g, w_router, b_router):
    n = x.shape[0]
    w_pad = jnp.pad(w_router, ((0, 0), (0, GROUP - N_EXPERTS)))
    b_pad = jnp.pad(b_router, (0, GROUP - N_EXPERTS)).reshape(1, GROUP)
    return pl.pallas_call(
        _router_kernel,
        grid=(n // TR,),
        in_specs=[pl.BlockSpec((TR, D_MODEL), lambda i: (i, 0)),
                  pl.BlockSpec((1, D_MODEL), lambda i: (0, 0)),
                  pl.BlockSpec((D_MODEL, GROUP), lambda i: (0, 0)),
                  pl.BlockSpec((1, GROUP), lambda i: (0, 0))],
        out_specs=[pl.BlockSpec((TR, GROUP), lambda i: (i, 0)),
                   pl.BlockSpec((TR, HALF), lambda i: (i, 0))],
        out_shape=[jax.ShapeDtypeStruct((n, GROUP), f32),
                   jax.ShapeDtypeStruct((n, HALF), jnp.uint32)],
        compiler_params=_params(1), name="router",
    )(x, g.reshape(1, D_MODEL), w_pad, b_pad)


def _start_row_gather(idx_ref, base, src_hbm, dst_ref, sem):
    def issue(r8, carry):
        for k in range(8):
            r = r8 * 8 + k
            pltpu.make_async_copy(src_hbm.at[pl.ds(idx_ref[base + r], 1)],
                                  dst_ref.at[pl.ds(r, 1)], sem).start(priority=k % 2)
        return carry
    lax.fori_loop(0, TG // 8, issue, 0)


def _wait_row_gather(src_hbm, dst_ref, sem):
    pltpu.make_async_copy(src_hbm.at[pl.ds(0, TG)], dst_ref, sem).wait()


def _dispatch_kernel(tok_ref, np_ref, x_hbm, o_ref, buf_ref, sem):
    i = pl.program_id(0)
    n = pl.num_programs(0)

    def has_rows(step):
        first_row = (step % STEPS_PER_BLOCK) * TG
        return first_row < np_ref[step // STEPS_PER_BLOCK] * PIECE

    def start(step):
        slot = step % 2
        _start_row_gather(tok_ref, step * TG, x_hbm, buf_ref.at[slot], sem.at[slot])

    @pl.when(jnp.logical_and(i == 0, has_rows(0)))
    def _():
        start(0)

    nxt = jnp.minimum(i + 1, n - 1)

    @pl.when(jnp.logical_and(i + 1 < n, has_rows(nxt)))
    def _():
        start(nxt)

    @pl.when(has_rows(i))
    def _():
        slot = i % 2
        _wait_row_gather(x_hbm, buf_ref.at[slot], sem.at[slot])
        _unpack_bf16_pairs(buf_ref[slot], o_ref)

    @pl.when(jnp.logical_not(has_rows(i)))
    def _():
        o_ref[...] = jnp.zeros_like(o_ref)


def dispatch(xn_packed, token_of_row, n_pieces):
    return pl.pallas_call(
        _dispatch_kernel,
        grid_spec=pltpu.PrefetchScalarGridSpec(
            num_scalar_prefetch=2,
            grid=(P_ROWS // TG,),
            in_specs=[pl.BlockSpec(memory_space=pl.ANY)],
            out_specs=pl.BlockSpec((TG, D_MODEL), lambda i, tok, npc: (i, 0)),
            scratch_shapes=[pltpu.VMEM((2, TG, HALF), jnp.uint32), pltpu.SemaphoreType.DMA((2,))]),
        out_shape=jax.ShapeDtypeStruct((P_ROWS, D_MODEL), bf16),
        compiler_params=_params(1), name="moe_dispatch",
    )(token_of_row, n_pieces, xn_packed)


def _moe_gate_up_kernel(be_ref, np_ref, a_ref, wg_ref, wu_ref, h_ref):
    n_pieces = np_ref[pl.program_id(0)]
    for q in range(PIECES + 1):
        @pl.when(n_pieces == q)
        def _():
            rows = q * PIECE
            if rows > 0:
                _swiglu_rows(a_ref, wg_ref, wu_ref, h_ref, rows)
            if rows < TM:
                h_ref[rows:TM, :] = jnp.zeros((TM - rows, TF), h_ref.dtype)


def _input_step(np_ref, b, s, n_inner):
    last = np_ref[N_MOE_BLOCKS] - 1
    return jnp.minimum(b, last), jnp.where(b <= last, s, n_inner - 1)


def moe_gate_up(a, wg_all, wu_all, j, block_expert, n_pieces):
    n_f = D_FF // TF

    def w_index(b, f, be, npc):
        _, ff = _input_step(npc, b, f, n_f)
        return j, be[b], 0, ff

    wspec = pl.BlockSpec((None, None, D_MODEL, TF), w_index)
    return pl.pallas_call(
        _moe_gate_up_kernel,
        grid_spec=pltpu.PrefetchScalarGridSpec(
            num_scalar_prefetch=2,
            grid=(N_MOE_BLOCKS, n_f),
            in_specs=[_resident_rows_spec(lambda b, f, be, npc: (_input_step(npc, b, f, n_f)[0], 0)),
                      wspec, wspec],
            out_specs=pl.BlockSpec((TM, TF), lambda b, f, be, npc: (b, f))),
        out_shape=jax.ShapeDtypeStruct((P_ROWS, D_FF), bf16),
        compiler_params=_params(2), name="moe_gate_up",
    )(block_expert, n_pieces, a, wg_all, wu_all)


def _down_accumulate_rows(h_ref, w_ref, o_ref, rows, first):
    if rows > 0:
        h = h_ref[0:rows, :]
        for c in range(D_MODEL // TN_PROJ):
            cs = slice(c * TN_PROJ, (c + 1) * TN_PROJ)
            part = _dot(h, w_ref[:, cs].astype(bf16))
            if first:
                o_ref[0:rows, cs] = part
            else:
                o_ref[0:rows, cs] += part
    if first and rows < TM:
        o_ref[rows:TM, :] = jnp.zeros((TM - rows, D_MODEL), o_ref.dtype)


def _moe_down_kernel(be_ref, np_ref, h_ref, w_ref, o_ref):
    n_pieces = np_ref[pl.program_id(0)]
    k = pl.program_id(1)
    for q in range(PIECES + 1):
        @pl.when(jnp.logical_and(n_pieces == q, k == 0))
        def _():
            _down_accumulate_rows(h_ref, w_ref, o_ref, q * PIECE, True)

        if q > 0:
            @pl.when(jnp.logical_and(n_pieces == q, k > 0))
            def _():
                _down_accumulate_rows(h_ref, w_ref, o_ref, q * PIECE, False)


def moe_down(h, wd_all, j, block_expert, n_pieces):
    n_k = D_FF // TK

    def w_index(b, k, be, npc):
        _, kk = _input_step(npc, b, k, n_k)
        return j, be[b], kk, 0

    return pl.pallas_call(
        _moe_down_kernel,
        grid_spec=pltpu.PrefetchScalarGridSpec(
            num_scalar_prefetch=2,
            grid=(N_MOE_BLOCKS, n_k),
            in_specs=[pl.BlockSpec((TM, TK), lambda b, k, be, npc: _input_step(npc, b, k, n_k)),
                      pl.BlockSpec((None, None, TK, D_MODEL), w_index)],
            out_specs=pl.BlockSpec((TM, D_MODEL), lambda b, k, be, npc: (b, 0))),
        out_shape=jax.ShapeDtypeStruct((P_ROWS, D_MODEL), f32),
        compiler_params=_params(2), name="moe_down",
    )(block_expert, n_pieces, h, wd_all)


def _combine_kernel(p1_ref, p2_ref, x_ref, r_ref, y_hbm, g_ref, op_ref, os_ref, buf_ref, sem):
    i = pl.program_id(0)
    n = pl.num_programs(0)

    def start(step):
        slot = step % 2
        _start_row_gather(p1_ref, step * TG, y_hbm, buf_ref.at[slot, 0], sem.at[slot])
        _start_row_gather(p2_ref, step * TG, y_hbm, buf_ref.at[slot, 1], sem.at[slot])

    @pl.when(i == 0)
    def _():
        start(0)

    @pl.when(i + 1 < n)
    def _():
        start(i + 1)

    slot = i % 2
    _wait_row_gather(y_hbm, buf_ref.at[slot, 0], sem.at[slot])
    _wait_row_gather(y_hbm, buf_ref.at[slot, 1], sem.at[slot])

    def result():
        x = x_ref[...] + (r_ref[:, 2:3] * buf_ref[slot, 0] + r_ref[:, 3:4] * buf_ref[slot, 1])
        return _rms(x, g_ref[...])

    is_prompt = i < N_PROMPT // TG

    @pl.when(is_prompt)
    def _():
        op_ref[...] = result()

    @pl.when(jnp.logical_not(is_prompt))
    def _():
        os_ref[...] = result()


def combine_final_norm(x, routing, y_rows, pos1, pos2, g):
    n_p = N_PROMPT // TG
    return pl.pallas_call(
        _combine_kernel,
        grid_spec=pltpu.PrefetchScalarGridSpec(
            num_scalar_prefetch=2,
            grid=(N_TOK // TG,),
            in_specs=[pl.BlockSpec((TG, D_MODEL), lambda i, p1, p2: (i, 0)),
                      pl.BlockSpec((TG, GROUP), lambda i, p1, p2: (i, 0)),
                      pl.BlockSpec(memory_space=pl.ANY),
                      pl.BlockSpec((1, D_MODEL), lambda i, p1, p2: (0, 0))],
            out_specs=[pl.BlockSpec((TG, D_MODEL), lambda i, p1, p2: (jnp.minimum(i, n_p - 1), 0)),
                       pl.BlockSpec((TG, D_MODEL), lambda i, p1, p2: (jnp.maximum(i - n_p, 0), 0))],
            scratch_shapes=[pltpu.VMEM((2, 2, TG, D_MODEL), f32), pltpu.SemaphoreType.DMA((2,))]),
        out_shape=[jax.ShapeDtypeStruct((N_PROMPT, D_MODEL), f32),
                   jax.ShapeDtypeStruct((N_SAMPLE, D_MODEL), f32)],
        compiler_params=_params(1), name="moe_combine",
    )(pos1, pos2, x, routing, y_rows, g.reshape(1, D_MODEL))


def _routing_tables(routing):
    experts = routing[:, :TOP_K].astype(jnp.int32).reshape(-1)
    onehot = (experts[:, None] == jnp.arange(N_EXPERTS, dtype=jnp.int32)[None, :]).astype(jnp.int32)
    csum = jnp.cumsum(onehot, axis=0)
    rank = jnp.sum(onehot * (csum - 1), axis=1)
    counts = csum[-1]
    blocks = (counts + TM - 1) // TM
    block_end = jnp.cumsum(blocks)
    block_start = block_end - blocks
    pos = (block_start[experts] * TM + rank).astype(jnp.int32)
    n_used = block_end[-1]
    blk = jnp.arange(N_MOE_BLOCKS, dtype=jnp.int32)
    block_expert = jnp.minimum(
        jnp.sum((jnp.minimum(blk, n_used - 1)[:, None] >= block_end[None, :]).astype(jnp.int32), axis=1),
        N_EXPERTS - 1).astype(jnp.int32)
    rows_in_block = jnp.clip(counts[block_expert] - (blk - block_start[block_expert]) * TM, 0, TM)
    rows_in_block = jnp.where(blk < n_used, rows_in_block, 0)
    n_pieces = jnp.concatenate([(rows_in_block + PIECE - 1) // PIECE, n_used[None]]).astype(jnp.int32)
    token = jnp.arange(TOP_K * N_TOK, dtype=jnp.int32) // TOP_K
    token_of_row = jnp.zeros((P_ROWS,), jnp.int32).at[pos].set(
        token, unique_indices=True, mode="promise_in_bounds")
    pos = pos.reshape(N_TOK, TOP_K)
    return token_of_row, pos[:, 0], pos[:, 1], block_expert, n_pieces


def kernel(x_prompt, x_sample, state_conv, ln1_g, w_in, conv_w, conv_b, v_ln_g, v_ln_b, smlp_w, smlp_b, out_norm_a_g, out_norm_b_g, w_out, ln2_g, ffn_w_gate, ffn_w_up, ffn_w_down, router_w, router_b, moe_w_gate, moe_w_up, moe_w_down, final_norm_g):
    assert DEPTH == 2
    x_p = x_prompt.reshape(N_PROMPT, D_MODEL)
    x_s = x_sample.reshape(N_SAMPLE, D_MODEL)
    xn = rms_norm_split(x_p, x_s, ln1_g[0])
    conv_p, conv_s, v_rows = [], [], []
    x = y_p = y_s = None
    for l in range(DEPTH):
        z = in_proj(xn, w_in, l)
        ymix, vn, tails = mixers(z, state_conv[l], conv_w[l], conv_b[l], v_ln_g[l], v_ln_b[l],
                                 smlp_w[l], smlp_b[l], out_norm_a_g[l], out_norm_b_g[l])
        if l == 0:
            x = out_proj_residual_split(ymix, w_out, l, x_p, x_s)
        else:
            x = out_proj_residual(ymix, w_out, l, x)
        conv_p.append(jnp.stack([tails[(BLOCKS_PER_SEQ * (b + 1) - 1) * 16 + 14:BLOCKS_PER_SEQ * (b + 1) * 16]
                                 for b in range(BATCH)]))
        conv_s.append(tails[N_PROMPT_BLOCKS * 16:].reshape(DEC_BATCH, 8, D_CONV)[:, 6:8])
        v_rows.append(vn.reshape(DEC_BATCH, DEC_SEQ, D_SMLP))
        j = l // 2
        if l % 2 == 0:
            xn2 = rms_norm(x, ln2_g[l], bf16)
            h = dense_gate_up(xn2, ffn_w_gate, ffn_w_up, j)
            d = dense_down(h, ffn_w_down, j)
            x, xn = add_rms_norm(x, d, ln1_g[l + 1])
        else:
            routing, xn_packed = router(x, ln2_g[l], router_w[j], router_b[j])
            token_of_row, pos1, pos2, block_expert, n_pieces = _routing_tables(routing)
            rows = dispatch(xn_packed, token_of_row, n_pieces)
            h = moe_gate_up(rows, moe_w_gate, moe_w_up, j, block_expert, n_pieces)
            y_rows = moe_down(h, moe_w_down, j, block_expert, n_pieces)
            y_p, y_s = combine_final_norm(x, routing, y_rows, pos1, pos2, final_norm_g)
    return (y_p.reshape(BATCH, SEQ, D_MODEL), y_s.reshape(DEC_BATCH, DEC_SEQ, D_MODEL),
            jnp.stack(conv_p), jnp.stack(conv_s), jnp.stack(v_rows))
```

```python
import jax
import jax.numpy as jnp
from jax import lax
from jax.experimental import pallas as pl
from jax.experimental.pallas import tpu as pltpu

D_MODEL = 4096
BATCH = 2
SEQ = 4096
DEPTH = 2
DEC_BATCH = 32
DEC_SEQ = 64
D_CONV = 2048
D_SMLP = 2048
N_GROUPS = 16
GROUP = 128
SMLP_CHUNK = 128
D_IN_PROJ = 3 * D_CONV + 2 * D_SMLP
D_FF = 14336
N_EXPERTS = 8
TOP_K = 2
EPS = 1e-5

N_PROMPT = BATCH * SEQ
N_SAMPLE = DEC_BATCH * DEC_SEQ
N_TOK = N_PROMPT + N_SAMPLE
ROWS = SMLP_CHUNK
N_PROMPT_BLOCKS = N_PROMPT // ROWS
N_SAMPLE_BLOCKS = N_SAMPLE // ROWS
N_BLOCKS = N_PROMPT_BLOCKS + N_SAMPLE_BLOCKS
BLOCKS_PER_SEQ = SEQ // ROWS

V7X_VMEM_LIMIT_BYTES = 60 * 1024 * 1024

TM = 1024
TN_PROJ = 512
TN_IN = 1024
TF = 512
TF_SLAB = 256
TK = 512
TR = 256
TG = 256
PIECE = 128
P_ROWS = TOP_K * N_TOK + N_EXPERTS * TM
N_MOE_BLOCKS = P_ROWS // TM
PIECES = TM // PIECE
STEPS_PER_BLOCK = TM // TG

bf16 = jnp.bfloat16
f32 = jnp.float32


def _params(n_axes):
    return pltpu.CompilerParams(dimension_semantics=("arbitrary",) * n_axes,
                                vmem_limit_bytes=V7X_VMEM_LIMIT_BYTES)


def _rms(x, g):
    ms = jnp.mean(x * x, axis=-1, keepdims=True)
    return x * lax.rsqrt(ms + EPS) * g


def _dot(a, w):
    return jnp.dot(a, w, preferred_element_type=f32)


def _rms_kernel(x_ref, g_ref, o_ref):
    o_ref[...] = _rms(x_ref[...], g_ref[...]).astype(o_ref.dtype)


def rms_norm(x, g, out_dtype):
    n = x.shape[0]
    return pl.pallas_call(
        _rms_kernel,
        grid=(n // TR,),
        in_specs=[pl.BlockSpec((TR, D_MODEL), lambda i: (i, 0)),
                  pl.BlockSpec((1, D_MODEL), lambda i: (0, 0))],
        out_specs=pl.BlockSpec((TR, D_MODEL), lambda i: (i, 0)),
        out_shape=jax.ShapeDtypeStruct((n, D_MODEL), out_dtype),
        compiler_params=_params(1), name="rms_norm",
    )(x, g.reshape(1, D_MODEL))


def _split_specs(rows, width):
    n_p = N_PROMPT // rows
    col = (lambda i, j: j) if width != D_MODEL else (lambda i, j: 0)
    prompt = lambda i, j=0: (jnp.minimum(i, n_p - 1), jnp.where(i < n_p, col(i, j), 0))
    sample = lambda i, j=0: (jnp.maximum(i - n_p, 0), jnp.where(i < n_p, 0, col(i, j)))
    return pl.BlockSpec((rows, width), prompt), pl.BlockSpec((rows, width), sample)


def _rms_split_kernel(xp_ref, xs_ref, g_ref, o_ref):
    is_prompt = pl.program_id(0) < N_PROMPT // TR

    @pl.when(is_prompt)
    def _():
        o_ref[...] = _rms(xp_ref[...], g_ref[...]).astype(o_ref.dtype)

    @pl.when(jnp.logical_not(is_prompt))
    def _():
        o_ref[...] = _rms(xs_ref[...], g_ref[...]).astype(o_ref.dtype)


def rms_norm_split(x_p, x_s, g):
    p_spec, s_spec = _split_specs(TR, D_MODEL)
    return pl.pallas_call(
        _rms_split_kernel,
        grid=(N_TOK // TR,),
        in_specs=[p_spec, s_spec, pl.BlockSpec((1, D_MODEL), lambda i: (0, 0))],
        out_specs=pl.BlockSpec((TR, D_MODEL), lambda i: (i, 0)),
        out_shape=jax.ShapeDtypeStruct((N_TOK, D_MODEL), bf16),
        compiler_params=_params(1), name="rms_norm_split",
    )(x_p, x_s, g.reshape(1, D_MODEL))


def _add_rms_kernel(x_ref, d_ref, g_ref, xo_ref, no_ref):
    x = x_ref[...] + d_ref[...]
    xo_ref[...] = x
    no_ref[...] = _rms(x, g_ref[...]).astype(no_ref.dtype)


def add_rms_norm(x, d, g):
    n = x.shape[0]
    row = pl.BlockSpec((TR, D_MODEL), lambda i: (i, 0))
    return pl.pallas_call(
        _add_rms_kernel,
        grid=(n // TR,),
        in_specs=[row, row, pl.BlockSpec((1, D_MODEL), lambda i: (0, 0))],
        out_specs=[row, row],
        out_shape=[jax.ShapeDtypeStruct((n, D_MODEL), f32),
                   jax.ShapeDtypeStruct((n, D_MODEL), bf16)],
        compiler_params=_params(1), name="add_rms_norm",
    )(x, d, g.reshape(1, D_MODEL))


def _proj_kernel(a_ref, w_ref, o_ref):
    a = a_ref[...]
    for c in range(TN_IN // TN_PROJ):
        cs = slice(c * TN_PROJ, (c + 1) * TN_PROJ)
        o_ref[:, cs] = _dot(a, w_ref[:, cs].astype(bf16))


def in_proj(a, w_all, layer):
    n = a.shape[0]
    return pl.pallas_call(
        _proj_kernel,
        grid=(n // TM, D_IN_PROJ // TN_IN),
        in_specs=[_resident_rows_spec(lambda i, j: (i, 0)),
                  pl.BlockSpec((None, D_MODEL, TN_IN), lambda i, j: (layer, 0, j))],
        out_specs=pl.BlockSpec((TM, TN_IN), lambda i, j: (i, j)),
        out_shape=jax.ShapeDtypeStruct((n, D_IN_PROJ), f32),
        compiler_params=_params(2), name="in_proj",
    )(a, w_all)


def _proj_res_kernel(a_ref, w_ref, x_ref, o_ref):
    o_ref[...] = x_ref[...] + _dot(a_ref[...], w_ref[...].astype(bf16))


def out_proj_residual(a, w_all, layer, x):
    n = a.shape[0]
    tile = pl.BlockSpec((TM, TN_PROJ), lambda i, j: (i, j))
    return pl.pallas_call(
        _proj_res_kernel,
        grid=(n // TM, D_MODEL // TN_PROJ),
        in_specs=[pl.BlockSpec((TM, D_MODEL), lambda i, j: (i, 0)),
                  pl.BlockSpec((None, D_MODEL, TN_PROJ), lambda i, j: (layer, 0, j)),
                  tile],
        out_specs=tile,
        out_shape=jax.ShapeDtypeStruct((n, D_MODEL), f32),
        compiler_params=_params(2), name="out_proj",
    )(a, w_all, x)


def _proj_res_split_kernel(a_ref, w_ref, xp_ref, xs_ref, o_ref):
    is_prompt = pl.program_id(0) < N_PROMPT // TM
    y = _dot(a_ref[...], w_ref[...].astype(bf16))

    @pl.when(is_prompt)
    def _():
        o_ref[...] = xp_ref[...] + y

    @pl.when(jnp.logical_not(is_prompt))
    def _():
        o_ref[...] = xs_ref[...] + y


def out_proj_residual_split(a, w_all, layer, x_p, x_s):
    p_spec, s_spec = _split_specs(TM, TN_PROJ)
    return pl.pallas_call(
        _proj_res_split_kernel,
        grid=(N_TOK // TM, D_MODEL // TN_PROJ),
        in_specs=[pl.BlockSpec((TM, D_MODEL), lambda i, j: (i, 0)),
                  pl.BlockSpec((None, D_MODEL, TN_PROJ), lambda i, j: (layer, 0, j)),
                  p_spec, s_spec],
        out_specs=pl.BlockSpec((TM, TN_PROJ), lambda i, j: (i, j)),
        out_shape=jax.ShapeDtypeStruct((N_TOK, D_MODEL), f32),
        compiler_params=_params(2), name="out_proj_split",
    )(a, w_all, x_p, x_s)


def _mixer_kernel(z_ref, st_ref, cw_ref, cb_ref, lg_ref, lb_ref, w_ref, sb_ref, ga_ref, gb_ref,
                  y_ref, vn_ref, tail_ref, prev_ref):
    i = pl.program_id(0)

    @pl.when(i == 0)
    def _():
        prev_ref[...] = jnp.zeros_like(prev_ref)

    is_sample = i >= N_PROMPT_BLOCKS
    sample_row = jnp.full((1, GROUP), i, jnp.int32) >= N_PROMPT_BLOCKS
    seq_start_row = (jnp.full((1, GROUP), i, jnp.int32) % BLOCKS_PER_SEQ) == 0
    mid = jnp.where(is_sample, DEC_SEQ, -8)
    row = lax.broadcasted_iota(jnp.int32, (ROWS, GROUP), 0)
    col = lax.broadcasted_iota(jnp.int32, (ROWS, GROUP), 1)
    sample_full = jnp.full((ROWS, GROUP), i, jnp.int32) >= N_PROMPT_BLOCKS
    same_seq = jnp.logical_or(jnp.logical_not(sample_full), (row // DEC_SEQ) == (col // DEC_SEQ))
    causal = jnp.logical_and(col <= row, same_seq)

    def group_rms(y, g):
        ms = jnp.mean(y * y, axis=-1, keepdims=True)
        return y * lax.rsqrt(ms + EPS) * g

    for grp in range(N_GROUPS):
        sl = slice(grp * GROUP, (grp + 1) * GROUP)
        gate_b = z_ref[:, grp * GROUP:(grp + 1) * GROUP]
        cin = (z_ref[:, D_CONV + grp * GROUP:D_CONV + (grp + 1) * GROUP]
               * z_ref[:, 2 * D_CONV + grp * GROUP:2 * D_CONV + (grp + 1) * GROUP])
        zero = jnp.zeros((1, GROUP), f32)
        a1 = jnp.where(sample_row, st_ref[0, 1:2, sl], jnp.where(seq_start_row, zero, prev_ref[7:8, sl]))
        a2 = jnp.where(sample_row, st_ref[0, 0:1, sl], jnp.where(seq_start_row, zero, prev_ref[6:7, sl]))
        b1 = st_ref[0, 3:4, sl]
        b2 = st_ref[0, 2:3, sl]
        s1 = pltpu.roll(cin, 1, 0)
        s2 = pltpu.roll(cin, 2, 0)
        s1 = jnp.where(row == 0, a1, s1)
        s1 = jnp.where(row == mid, b1, s1)
        s2 = jnp.where(row == 0, a2, jnp.where(row == 1, a1, s2))
        s2 = jnp.where(row == mid, b2, jnp.where(row == mid + 1, b1, s2))
        conv = cb_ref[:, sl] + cw_ref[0:1, sl] * s2 + cw_ref[1:2, sl] * s1 + cw_ref[2:3, sl] * cin
        y_ref[:, sl] = group_rms(gate_b * conv, ga_ref[:, sl]).astype(y_ref.dtype)
        tail_ref[0:8, sl] = cin[DEC_SEQ - 8:DEC_SEQ]
        tail_ref[8:16, sl] = cin[ROWS - 8:ROWS]
        prev_ref[:, sl] = cin[ROWS - 8:ROWS]

    v0 = 3 * D_CONV + D_SMLP
    u0 = 3 * D_CONV
    acc = z_ref[:, v0:v0 + GROUP]
    for grp in range(1, N_GROUPS):
        acc = acc + z_ref[:, v0 + grp * GROUP:v0 + (grp + 1) * GROUP]
    mean = jnp.sum(acc, axis=-1, keepdims=True) * (1.0 / D_SMLP)
    acc = jnp.zeros((ROWS, GROUP), f32)
    for grp in range(N_GROUPS):
        vc = z_ref[:, v0 + grp * GROUP:v0 + (grp + 1) * GROUP] - mean
        acc = acc + vc * vc
    rstd = lax.rsqrt(jnp.sum(acc, axis=-1, keepdims=True) * (1.0 / D_SMLP) + EPS)
    for grp in range(N_GROUPS):
        sl = slice(grp * GROUP, (grp + 1) * GROUP)
        vc = z_ref[:, v0 + grp * GROUP:v0 + (grp + 1) * GROUP] - mean
        vn = vc * rstd * lg_ref[:, sl] + lb_ref[:, sl]

        vn_ref[:, sl] = vn
        w = jnp.where(causal, w_ref[grp], 0.0).astype(bf16)
        s = _dot(w, vn.astype(bf16)) + sb_ref[:, sl]
        y_b = z_ref[:, u0 + grp * GROUP:u0 + (grp + 1) * GROUP] * s
        y_ref[:, D_CONV + grp * GROUP:D_CONV + (grp + 1) * GROUP] = (
            group_rms(y_b, gb_ref[:, sl]).astype(y_ref.dtype))


def mixers(z, state, conv_w, conv_b, ln_g, ln_b, smlp_w, smlp_b, g_a, g_b):
    half = DEC_SEQ
    w_modes = jnp.stack([smlp_w, jnp.tile(smlp_w[:, :half, :half], (1, 2, 2))])
    b_rows = jnp.stack([smlp_b.T, jnp.tile(smlp_b[:, :half], (1, 2)).T])
    b_modes = jnp.repeat(b_rows, GROUP, axis=2)
    st = state.reshape(N_SAMPLE_BLOCKS, 4, D_CONV)
    vec = lambda a: a.reshape(1, -1)
    vspec = pl.BlockSpec((1, D_CONV), lambda i: (0, 0))
    mode = lambda i: i // N_PROMPT_BLOCKS
    sblk = lambda i: jnp.maximum(i - N_PROMPT_BLOCKS, 0)
    return pl.pallas_call(
        _mixer_kernel,
        grid=(N_BLOCKS,),
        in_specs=[pl.BlockSpec((ROWS, D_IN_PROJ), lambda i: (i, 0)),
                  pl.BlockSpec((1, 4, D_CONV), lambda i: (sblk(i), 0, 0)),
                  pl.BlockSpec((3, D_CONV), lambda i: (0, 0)),
                  vspec, vspec, vspec,
                  pl.BlockSpec((None, N_GROUPS, ROWS, ROWS), lambda i: (mode(i), 0, 0, 0)),
                  pl.BlockSpec((None, ROWS, D_SMLP), lambda i: (mode(i), 0, 0)),
                  vspec, vspec],
        out_specs=[pl.BlockSpec((ROWS, 2 * D_CONV), lambda i: (i, 0)),
                   pl.BlockSpec((ROWS, D_SMLP), lambda i: (sblk(i), 0)),
                   pl.BlockSpec((16, D_CONV), lambda i: (i, 0))],
        out_shape=[jax.ShapeDtypeStruct((N_TOK, 2 * D_CONV), bf16),
                   jax.ShapeDtypeStruct((N_SAMPLE, D_SMLP), f32),
                   jax.ShapeDtypeStruct((N_BLOCKS * 16, D_CONV), f32)],
        scratch_shapes=[pltpu.VMEM((8, D_CONV), f32)],
        compiler_params=_params(1), name="mixers",
    )(z, st, conv_w, vec(conv_b), vec(ln_g), vec(ln_b), w_modes, b_modes, vec(g_a), vec(g_b))


def _swiglu_rows(a_ref, wg_ref, wu_ref, h_ref, rows):
    a = a_ref[0:rows, :]
    for c in range(TF // TF_SLAB):
        cs = slice(c * TF_SLAB, (c + 1) * TF_SLAB)
        g = _dot(a, wg_ref[:, cs].astype(bf16))
        u = _dot(a, wu_ref[:, cs].astype(bf16))
        h_ref[0:rows, cs] = (g * jax.nn.sigmoid(g) * u).astype(h_ref.dtype)


def _gate_up_kernel(a_ref, wg_ref, wu_ref, h_ref):
    _swiglu_rows(a_ref, wg_ref, wu_ref, h_ref, TM)


def _resident_rows_spec(index_map):
    return pl.BlockSpec((TM, D_MODEL), index_map, pipeline_mode=pl.Buffered(1))


def dense_gate_up(a, wg_all, wu_all, j):
    n = a.shape[0]
    wspec = pl.BlockSpec((None, D_MODEL, TF), lambda i, f: (j, 0, f))
    return pl.pallas_call(
        _gate_up_kernel,
        grid=(n // TM, D_FF // TF),
        in_specs=[_resident_rows_spec(lambda i, f: (i, 0)), wspec, wspec],
        out_specs=pl.BlockSpec((TM, TF), lambda i, f: (i, f)),
        out_shape=jax.ShapeDtypeStruct((n, D_FF), bf16),
        compiler_params=_params(2), name="dense_gate_up",
    )(a, wg_all, wu_all)


def _down_accumulate(h_ref, w_ref, o_ref, first):
    h = h_ref[...]
    for c in range(D_MODEL // TN_PROJ):
        cs = slice(c * TN_PROJ, (c + 1) * TN_PROJ)
        part = _dot(h, w_ref[:, cs].astype(bf16))
        if first:
            o_ref[:, cs] = part
        else:
            o_ref[:, cs] += part


def _down_kernel(h_ref, w_ref, o_ref):
    @pl.when(pl.program_id(1) == 0)
    def _():
        _down_accumulate(h_ref, w_ref, o_ref, True)

    @pl.when(pl.program_id(1) > 0)
    def _():
        _down_accumulate(h_ref, w_ref, o_ref, False)


def dense_down(h, wd_all, j):
    n = h.shape[0]
    return pl.pallas_call(
        _down_kernel,
        grid=(n // TM, D_FF // TK),
        in_specs=[pl.BlockSpec((TM, TK), lambda i, k: (i, k)),
                  pl.BlockSpec((None, TK, D_MODEL), lambda i, k: (j, k, 0))],
        out_specs=pl.BlockSpec((TM, D_MODEL), lambda i, k: (i, 0)),
        out_shape=jax.ShapeDtypeStruct((n, D_MODEL), f32),
        compiler_params=_params(2), name="dense_down",
    )(h, wd_all)


HALF = D_MODEL // 2
HIGH16 = 0xFFFF0000


def _pack_bf16_pairs(xn):
    lo = lax.bitcast_convert_type(xn[:, :HALF].astype(f32), jnp.uint32) >> 16
    hi = lax.bitcast_convert_type(xn[:, HALF:].astype(f32), jnp.uint32) & jnp.uint32(HIGH16)
    return hi | lo


def _unpack_bf16_pairs(words, o_ref):
    o_ref[:, :HALF] = lax.bitcast_convert_type(words << 16, f32).astype(o_ref.dtype)
    o_ref[:, HALF:] = lax.bitcast_convert_type(words & jnp.uint32(HIGH16), f32).astype(o_ref.dtype)


def _router_kernel(x_ref, g_ref, w_ref, b_ref, o_ref, p_ref):
    xn = _rms(x_ref[...], g_ref[...]).astype(bf16)
    p_ref[...] = _pack_bf16_pairs(xn)
    logits = _dot(xn, w_ref[...].astype(bf16)) + b_ref[...]
    lane = lax.broadcasted_iota(jnp.int32, logits.shape, 1)
    neg = jnp.float32(-jnp.inf)
    logits = jnp.where(lane < N_EXPERTS, logits, neg)
    m1 = jnp.max(logits, axis=-1, keepdims=True)
    i1 = jnp.min(jnp.where(logits == m1, lane, GROUP), axis=-1, keepdims=True)
    rest = jnp.where(lane == i1, neg, logits)
    m2 = jnp.max(rest, axis=-1, keepdims=True)
    i2 = jnp.min(jnp.where(rest == m2, lane, GROUP), axis=-1, keepdims=True)
    e = jnp.exp(m2 - m1)
    w1 = 1.0 / (1.0 + e)
    w2 = e / (1.0 + e)
    out = jnp.where(lane == 0, i1.astype(f32),
                    jnp.where(lane == 1, i2.astype(f32),
                              jnp.where(lane == 2, w1, jnp.where(lane == 3, w2, 0.0))))
    o_ref[...] = out


def router(x, g, w_router, b_router):
    n = x.shape[0]
    w_pad = jnp.pad(w_router, ((0, 0), (0, GROUP - N_EXPERTS)))
    b_pad = jnp.pad(b_router, (0, GROUP - N_EXPERTS)).reshape(1, GROUP)
    return pl.pallas_call(
        _router_kernel,
        grid=(n // TR,),
        in_specs=[pl.BlockSpec((TR, D_MODEL), lambda i: (i, 0)),
                  pl.BlockSpec((1, D_MODEL), lambda i: (0, 0)),
                  pl.BlockSpec((D_MODEL, GROUP), lambda i: (0, 0)),
                  pl.BlockSpec((1, GROUP), lambda i: (0, 0))],
        out_specs=[pl.BlockSpec((TR, GROUP), lambda i: (i, 0)),
                   pl.BlockSpec((TR, HALF), lambda i: (i, 0))],
        out_shape=[jax.ShapeDtypeStruct((n, GROUP), f32),
                   jax.ShapeDtypeStruct((n, HALF), jnp.uint32)],
        compiler_params=_params(1), name="router",
    )(x, g.reshape(1, D_MODEL), w_pad, b_pad)


def _start_row_gather(idx_ref, base, src_hbm, dst_ref, sem):
    def issue(r8, carry):
        for k in range(8):
            r = r8 * 8 + k
            pltpu.make_async_copy(src_hbm.at[pl.ds(idx_ref[base + r], 1)],
                                  dst_ref.at[pl.ds(r, 1)], sem).start(priority=k % 2)
        return carry
    lax.fori_loop(0, TG // 8, issue, 0)


def _wait_row_gather(src_hbm, dst_ref, sem):
    pltpu.make_async_copy(src_hbm.at[pl.ds(0, TG)], dst_ref, sem).wait()


def _dispatch_kernel(tok_ref, np_ref, x_hbm, o_ref, buf_ref, sem):
    i = pl.program_id(0)
    n = pl.num_programs(0)

    def has_rows(step):
        first_row = (step % STEPS_PER_BLOCK) * TG
        return first_row < np_ref[step // STEPS_PER_BLOCK] * PIECE

    def start(step):
        slot = step % 2
        _start_row_gather(tok_ref, step * TG, x_hbm, buf_ref.at[slot], sem.at[slot])

    @pl.when(jnp.logical_and(i == 0, has_rows(0)))
    def _():
        start(0)

    nxt = jnp.minimum(i + 1, n - 1)

    @pl.when(jnp.logical_and(i + 1 < n, has_rows(nxt)))
    def _():
        start(nxt)

    @pl.when(has_rows(i))
    def _():
        slot = i % 2
        _wait_row_gather(x_hbm, buf_ref.at[slot], sem.at[slot])
        _unpack_bf16_pairs(buf_ref[slot], o_ref)

    @pl.when(jnp.logical_not(has_rows(i)))
    def _():
        o_ref[...] = jnp.zeros_like(o_ref)


def dispatch(xn_packed, token_of_row, n_pieces):
    return pl.pallas_call(
        _dispatch_kernel,
        grid_spec=pltpu.PrefetchScalarGridSpec(
            num_scalar_prefetch=2,
            grid=(P_ROWS // TG,),
            in_specs=[pl.BlockSpec(memory_space=pl.ANY)],
            out_specs=pl.BlockSpec((TG, D_MODEL), lambda i, tok, npc: (i, 0)),
            scratch_shapes=[pltpu.VMEM((2, TG, HALF), jnp.uint32), pltpu.SemaphoreType.DMA((2,))]),
        out_shape=jax.ShapeDtypeStruct((P_ROWS, D_MODEL), bf16),
        compiler_params=_params(1), name="moe_dispatch",
    )(token_of_row, n_pieces, xn_packed)


def _moe_gate_up_kernel(be_ref, np_ref, a_ref, wg_ref, wu_ref, h_ref):
    n_pieces = np_ref[pl.program_id(0)]
    for q in range(PIECES + 1):
        @pl.when(n_pieces == q)
        def _():
            rows = q * PIECE
            if rows > 0:
                _swiglu_rows(a_ref, wg_ref, wu_ref, h_ref, rows)
            if rows < TM:
                h_ref[rows:TM, :] = jnp.zeros((TM - rows, TF), h_ref.dtype)


def _input_step(np_ref, b, s, n_inner):
    last = np_ref[N_MOE_BLOCKS] - 1
    return jnp.minimum(b, last), jnp.where(b <= last, s, n_inner - 1)


def moe_gate_up(a, wg_all, wu_all, j, block_expert, n_pieces):
    n_f = D_FF // TF

    def w_index(b, f, be, npc):
        _, ff = _input_step(npc, b, f, n_f)
        return j, be[b], 0, ff

    wspec = pl.BlockSpec((None, None, D_MODEL, TF), w_index)
    return pl.pallas_call(
        _moe_gate_up_kernel,
        grid_spec=pltpu.PrefetchScalarGridSpec(
            num_scalar_prefetch=2,
            grid=(N_MOE_BLOCKS, n_f),
            in_specs=[_resident_rows_spec(lambda b, f, be, npc: (_input_step(npc, b, f, n_f)[0], 0)),
                      wspec, wspec],
            out_specs=pl.BlockSpec((TM, TF), lambda b, f, be, npc: (b, f))),
        out_shape=jax.ShapeDtypeStruct((P_ROWS, D_FF), bf16),
        compiler_params=_params(2), name="moe_gate_up",
    )(block_expert, n_pieces, a, wg_all, wu_all)


def _down_accumulate_rows(h_ref, w_ref, o_ref, rows, first):
    if rows > 0:
        h = h_ref[0:rows, :]
        for c in range(D_MODEL // TN_PROJ):
            cs = slice(c * TN_PROJ, (c + 1) * TN_PROJ)
            part = _dot(h, w_ref[:, cs].astype(bf16))
            if first:
                o_ref[0:rows, cs] = part
            else:
                o_ref[0:rows, cs] += part
    if first and rows < TM:
        o_ref[rows:TM, :] = jnp.zeros((TM - rows, D_MODEL), o_ref.dtype)


def _moe_down_kernel(be_ref, np_ref, h_ref, w_ref, o_ref):
    n_steps = (np_ref[pl.program_id(0)] * PIECE + TG - 1) // TG
    k = pl.program_id(1)
    for q in range(STEPS_PER_BLOCK + 1):
        @pl.when(jnp.logical_and(n_steps == q, k == 0))
        def _():
            _down_accumulate_rows(h_ref, w_ref, o_ref, q * TG, True)

        if q > 0:
            @pl.when(jnp.logical_and(n_steps == q, k > 0))
            def _():
                _down_accumulate_rows(h_ref, w_ref, o_ref, q * TG, False)


def moe_down(h, wd_all, j, block_expert, n_pieces):
    n_k = D_FF // TK

    def w_index(b, k, be, npc):
        _, kk = _input_step(npc, b, k, n_k)
        return j, be[b], kk, 0

    return pl.pallas_call(
        _moe_down_kernel,
        grid_spec=pltpu.PrefetchScalarGridSpec(
            num_scalar_prefetch=2,
            grid=(N_MOE_BLOCKS, n_k),
            in_specs=[pl.BlockSpec((TM, TK), lambda b, k, be, npc: _input_step(npc, b, k, n_k)),
                      pl.BlockSpec((None, None, TK, D_MODEL), w_index)],
            out_specs=pl.BlockSpec((TM, D_MODEL), lambda b, k, be, npc: (b, 0))),
        out_shape=jax.ShapeDtypeStruct((P_ROWS, D_MODEL), f32),
        compiler_params=_params(2), name="moe_down",
    )(block_expert, n_pieces, h, wd_all)


def _combine_kernel(p1_ref, p2_ref, x_ref, r_ref, y_hbm, g_ref, op_ref, os_ref, buf_ref, sem):
    i = pl.program_id(0)
    n = pl.num_programs(0)

    def start(step):
        slot = step % 2
        _start_row_gather(p1_ref, step * TG, y_hbm, buf_ref.at[slot, 0], sem.at[slot])
        _start_row_gather(p2_ref, step * TG, y_hbm, buf_ref.at[slot, 1], sem.at[slot])

    @pl.when(i == 0)
    def _():
        start(0)

    @pl.when(i + 1 < n)
    def _():
        start(i + 1)

    slot = i % 2
    _wait_row_gather(y_hbm, buf_ref.at[slot, 0], sem.at[slot])
    _wait_row_gather(y_hbm, buf_ref.at[slot, 1], sem.at[slot])

    def result():
        x = x_ref[...] + (r_ref[:, 2:3] * buf_ref[slot, 0] + r_ref[:, 3:4] * buf_ref[slot, 1])
        return _rms(x, g_ref[...])

    is_prompt = i < N_PROMPT // TG

    @pl.when(is_prompt)
    def _():
        op_ref[...] = result()

    @pl.when(jnp.logical_not(is_prompt))
    def _():
        os_ref[...] = result()


def combine_final_norm(x, routing, y_rows, pos1, pos2, g):
    n_p = N_PROMPT // TG
    return pl.pallas_call(
        _combine_kernel,
        grid_spec=pltpu.PrefetchScalarGridSpec(
            num_scalar_prefetch=2,
            grid=(N_TOK // TG,),
            in_specs=[pl.BlockSpec((TG, D_MODEL), lambda i, p1, p2: (i, 0)),
                      pl.BlockSpec((TG, GROUP), lambda i, p1, p2: (i, 0)),
                      pl.BlockSpec(memory_space=pl.ANY),
                      pl.BlockSpec((1, D_MODEL), lambda i, p1, p2: (0, 0))],
            out_specs=[pl.BlockSpec((TG, D_MODEL), lambda i, p1, p2: (jnp.minimum(i, n_p - 1), 0)),
                       pl.BlockSpec((TG, D_MODEL), lambda i, p1, p2: (jnp.maximum(i - n_p, 0), 0))],
            scratch_shapes=[pltpu.VMEM((2, 2, TG, D_MODEL), f32), pltpu.SemaphoreType.DMA((2,))]),
        out_shape=[jax.ShapeDtypeStruct((N_PROMPT, D_MODEL), f32),
                   jax.ShapeDtypeStruct((N_SAMPLE, D_MODEL), f32)],
        compiler_params=_params(1), name="moe_combine",
    )(pos1, pos2, x, routing, y_rows, g.reshape(1, D_MODEL))


def _routing_tables(routing):
    experts = routing[:, :TOP_K].astype(jnp.int32).reshape(-1)
    onehot = (experts[:, None] == jnp.arange(N_EXPERTS, dtype=jnp.int32)[None, :]).astype(jnp.int32)
    csum = jnp.cumsum(onehot, axis=0)
    rank = jnp.sum(onehot * (csum - 1), axis=1)
    counts = csum[-1]
    blocks = (counts + TM - 1) // TM
    block_end = jnp.cumsum(blocks)
    block_start = block_end - blocks
    pos = (block_start[experts] * TM + rank).astype(jnp.int32)
    n_used = block_end[-1]
    blk = jnp.arange(N_MOE_BLOCKS, dtype=jnp.int32)
    block_expert = jnp.minimum(
        jnp.sum((jnp.minimum(blk, n_used - 1)[:, None] >= block_end[None, :]).astype(jnp.int32), axis=1),
        N_EXPERTS - 1).astype(jnp.int32)
    rows_in_block = jnp.clip(counts[block_expert] - (blk - block_start[block_expert]) * TM, 0, TM)
    rows_in_block = jnp.where(blk < n_used, rows_in_block, 0)
    n_pieces = jnp.concatenate([(rows_in_block + PIECE - 1) // PIECE, n_used[None]]).astype(jnp.int32)
    token = jnp.arange(TOP_K * N_TOK, dtype=jnp.int32) // TOP_K
    token_of_row = jnp.zeros((P_ROWS,), jnp.int32).at[pos].set(
        token, unique_indices=True, mode="promise_in_bounds")
    pos = pos.reshape(N_TOK, TOP_K)
    return token_of_row, pos[:, 0], pos[:, 1], block_expert, n_pieces


def kernel(x_prompt, x_sample, state_conv, ln1_g, w_in, conv_w, conv_b, v_ln_g, v_ln_b, smlp_w, smlp_b, out_norm_a_g, out_norm_b_g, w_out, ln2_g, ffn_w_gate, ffn_w_up, ffn_w_down, router_w, router_b, moe_w_gate, moe_w_up, moe_w_down, final_norm_g):
    assert DEPTH == 2
    x_p = x_prompt.reshape(N_PROMPT, D_MODEL)
    x_s = x_sample.reshape(N_SAMPLE, D_MODEL)
    xn = rms_norm_split(x_p, x_s, ln1_g[0])
    conv_p, conv_s, v_rows = [], [], []
    x = y_p = y_s = None
    for l in range(DEPTH):
        z = in_proj(xn, w_in, l)
        ymix, vn, tails = mixers(z, state_conv[l], conv_w[l], conv_b[l], v_ln_g[l], v_ln_b[l],
                                 smlp_w[l], smlp_b[l], out_norm_a_g[l], out_norm_b_g[l])
        if l == 0:
            x = out_proj_residual_split(ymix, w_out, l, x_p, x_s)
        else:
            x = out_proj_residual(ymix, w_out, l, x)
        conv_p.append(jnp.stack([tails[(BLOCKS_PER_SEQ * (b + 1) - 1) * 16 + 14:BLOCKS_PER_SEQ * (b + 1) * 16]
                                 for b in range(BATCH)]))
        conv_s.append(tails[N_PROMPT_BLOCKS * 16:].reshape(DEC_BATCH, 8, D_CONV)[:, 6:8])
        v_rows.append(vn.reshape(DEC_BATCH, DEC_SEQ, D_SMLP))
        j = l // 2
        if l % 2 == 0:
            xn2 = rms_norm(x, ln2_g[l], bf16)
            h = dense_gate_up(xn2, ffn_w_gate, ffn_w_up, j)
            d = dense_down(h, ffn_w_down, j)
            x, xn = add_rms_norm(x, d, ln1_g[l + 1])
        else:
            routing, xn_packed = router(x, ln2_g[l], router_w[j], router_b[j])
            token_of_row, pos1, pos2, block_expert, n_pieces = _routing_tables(routing)
            rows = dispatch(xn_packed, token_of_row, n_pieces)
            h = moe_gate_up(rows, moe_w_gate, moe_w_up, j, block_expert, n_pieces)
            y_rows = moe_down(h, moe_w_down, j, block_expert, n_pieces)
            y_p, y_s = combine_final_norm(x, routing, y_rows, pos1, pos2, final_norm_g)
    return (y_p.reshape(BATCH, SEQ, D_MODEL), y_s.reshape(DEC_BATCH, DEC_SEQ, D_MODEL),
            jnp.stack(conv_p), jnp.stack(conv_s), jnp.stack(v_rows))
```
